```python
import math
import jax, jax.numpy as jnp
from jax import lax
import numpy as np

D_MODEL = 1024
BATCH = 16
SEQ = 4096
DEPTH = 4

N_MIXERS = 2
N_ATTN_LAYERS = (DEPTH + 1) // 2
N_POOL_LAYERS = DEPTH // 2
EPS = 1e-6
HEAD_DIM = 64
N_HEADS = D_MODEL // (2 * HEAD_DIM)
ATTN_WIDTH = N_HEADS * 2 * HEAD_DIM
Q_BLOCK = 128
NUM_BUCKETS = 32
MAX_EXACT = NUM_BUCKETS // 2
MAX_DISTANCE = 128
POOL_WINDOWS = (2, 4, 8, 16)
N_POOL_GROUPS = len(POOL_WINDOWS)
GROUP_WIDTH = D_MODEL // N_POOL_GROUPS
N_EXPERT_GROUPS = 4
EXPERTS_PER_GROUP = 8
N_EXPERTS = N_EXPERT_GROUPS * EXPERTS_PER_GROUP
TOP_K = 2
EXPERT_FF = D_MODEL // 2
EXPERT_BLOCK = 128

kernel_name = "hybrid_diffattn_pool_hmoe"


def rms_norm(x, gain):
    xf = x.astype(jnp.float32)
    y = xf * lax.rsqrt(jnp.mean(xf * xf, axis=-1, keepdims=True) + EPS)
    return (y * gain.astype(jnp.float32)).astype(x.dtype)


def rel_bucket(dist):
    n = jnp.maximum(dist, 0)
    nf = jnp.maximum(n, 1).astype(jnp.float32)
    large = MAX_EXACT + (jnp.log(nf / MAX_EXACT) / math.log(MAX_DISTANCE / MAX_EXACT)
                         * (NUM_BUCKETS - MAX_EXACT)).astype(jnp.int32)
    large = jnp.minimum(large, NUM_BUCKETS - 1)
    return jnp.where(n < MAX_EXACT, n, large)


def diff_attention(h, w_qkv, q_gain, k_gain, lq1, lk1, lq2, lk2, subln_gain, w_o, rel_bias, lambda_init):
    B, S, _ = h.shape
    qkv = h @ w_qkv
    q, k, v = jnp.split(qkv, 3, axis=-1)
    q = rms_norm(q.reshape(B, S, N_HEADS, 2, HEAD_DIM), q_gain) * (HEAD_DIM ** -0.5)
    k = rms_norm(k.reshape(B, S, N_HEADS, 2, HEAD_DIM), k_gain)
    v = v.reshape(B, S, N_HEADS, 2 * HEAD_DIM)
    lam = (jnp.exp(jnp.sum(lq1.astype(jnp.float32) * lk1.astype(jnp.float32)))
           - jnp.exp(jnp.sum(lq2.astype(jnp.float32) * lk2.astype(jnp.float32))) + lambda_init)
    n_blk = S // Q_BLOCK
    qb = q.reshape(B, n_blk, Q_BLOCK, N_HEADS, 2, HEAD_DIM).transpose(1, 0, 2, 3, 4, 5)
    k_pos = jnp.arange(S)

    def one_block(args):
        qi, bi = args
        q_pos = bi * Q_BLOCK + jnp.arange(Q_BLOCK)
        dist = q_pos[:, None] - k_pos[None, :]
        bias = rel_bias.astype(jnp.float32)[rel_bucket(dist)].transpose(2, 0, 1)
        logits = jnp.einsum('bqhcd,bkhcd->bhcqk', qi, k).astype(jnp.float32) + bias[None, :, None]
        logits = jnp.where((dist >= 0)[None, None, None], logits, -jnp.inf)
        p = jax.nn.softmax(logits, axis=-1)
        a = p[:, :, 0] - lam * p[:, :, 1]
        return jnp.einsum('bhqk,bkhe->bqhe', a.astype(v.dtype), v)

    o = lax.map(one_block, (qb, jnp.arange(n_blk)))
    o = o.transpose(1, 0, 2, 3, 4).reshape(B, S, N_HEADS, 2 * HEAD_DIM)
    o = rms_norm(o, subln_gain) * (1.0 - lambda_init)
    return o.reshape(B, S, ATTN_WIDTH) @ w_o


def multiscale_pool(h, w_in, w_group, w_out, scale):
    B, S, _ = h.shape
    u = (h @ w_in).reshape(B, S, N_POOL_GROUPS, GROUP_WIDTH)
    uf = u.astype(jnp.float32)
    csum = jnp.cumsum(uf, axis=1)
    pos1 = jnp.arange(1, S + 1)
    outs = []
    for g, w in enumerate(POOL_WINDOWS):
        c = csum[:, :, g]
        shifted = jnp.pad(c, ((0, 0), (w, 0), (0, 0)))[:, :S]
        cnt = jnp.minimum(pos1, w).astype(jnp.float32)[None, :, None]
        outs.append((c - shifted) / cnt - uf[:, :, g])
    pooled = jnp.stack(outs, axis=2).astype(h.dtype)
    mixed = jnp.einsum('bsgc,gce->bsge', pooled, w_group).reshape(B, S, D_MODEL)
    return (mixed @ w_out) * scale


def routed_experts(xf, expert, weight, w_gate, w_up, w_down):
    T, D = xf.shape
    A = T * TOP_K
    e_flat = expert.reshape(A)
    tok = jnp.arange(A, dtype=jnp.int32) // TOP_K
    wt = weight.reshape(A)
    order = jnp.argsort(e_flat)
    e_sorted = e_flat[order]
    counts = jnp.bincount(e_flat, length=N_EXPERTS)
    padded = (counts + EXPERT_BLOCK - 1) // EXPERT_BLOCK * EXPERT_BLOCK
    pad_end = jnp.cumsum(padded)
    pad_start = pad_end - padded
    start = jnp.cumsum(counts) - counts
    dest = pad_start[e_sorted] + jnp.arange(A) - start[e_sorted]
    n_blocks = -(-A // EXPERT_BLOCK) + N_EXPERTS
    P = n_blocks * EXPERT_BLOCK
    slot_tok = jnp.full((P,), T, jnp.int32).at[dest].set(tok[order])
    slot_w = jnp.zeros((P,), jnp.float32).at[dest].set(wt[order])
    block_expert = jnp.minimum(
        jnp.searchsorted(pad_end, jnp.arange(n_blocks) * EXPERT_BLOCK, side='right'), N_EXPERTS - 1)
    x_pad = jnp.concatenate([xf, jnp.zeros((1, D), xf.dtype)], axis=0)
    xb = x_pad[slot_tok].reshape(n_blocks, EXPERT_BLOCK, D)

    def expert_block(args):
        xe, e = args
        return (jax.nn.silu(xe @ w_gate[e]) * (xe @ w_up[e])) @ w_down[e]

    yb = lax.map(expert_block, (xb, block_expert)).reshape(P, D)
    out = jnp.zeros((T + 1, D), jnp.float32).at[slot_tok].add(yb.astype(jnp.float32) * slot_w[:, None])
    return out[:T].astype(xf.dtype)


def hier_moe(h, wg1, bg1, wg2, bg2, w_gate, w_up, w_down):
    B, S, D = h.shape
    T = B * S
    xf = h.reshape(T, D)
    g_prob = jax.nn.softmax((xf @ wg1).astype(jnp.float32) + bg1.astype(jnp.float32), axis=-1)
    gp, gidx = lax.top_k(g_prob, 1)
    e_logit = ((xf @ wg2).astype(jnp.float32) + bg2.astype(jnp.float32)).reshape(T, N_EXPERT_GROUPS, EXPERTS_PER_GROUP)
    e_sel = jnp.take_along_axis(e_logit, gidx[:, :, None], axis=1)[:, 0]
    ev, eidx = lax.top_k(e_sel, TOP_K)
    weight = jax.nn.softmax(ev, axis=-1) * gp
    expert = gidx * EXPERTS_PER_GROUP + eidx
    return routed_experts(xf, expert, weight, w_gate, w_up, w_down).reshape(B, S, D)


def setup_inputs(seed: int = 0) -> dict:
    key = jax.random.key(seed)
    ks = jax.random.split(key, 32)
    f32 = jnp.float32
    nrm = lambda k, shp, s: jax.random.normal(k, shp, f32) * s
    nA, nP, L = N_ATTN_LAYERS, N_POOL_LAYERS, DEPTH
    return {
        "x": nrm(ks[0], (BATCH, SEQ, D_MODEL), 1.0),
        "rel_bias": nrm(ks[1], (NUM_BUCKETS, N_HEADS), 0.5),
        "attn_norm": 1.0 + nrm(ks[2], (nA, D_MODEL), 0.05),
        "w_qkv": nrm(ks[3], (nA, D_MODEL, 3 * ATTN_WIDTH), D_MODEL ** -0.5),
        "q_gain": 1.0 + nrm(ks[4], (nA, HEAD_DIM), 0.05),
        "k_gain": 1.0 + nrm(ks[5], (nA, HEAD_DIM), 0.05),
        "lambda_q1": nrm(ks[6], (nA, HEAD_DIM), 0.1),
        "lambda_k1": nrm(ks[7], (nA, HEAD_DIM), 0.1),
        "lambda_q2": nrm(ks[8], (nA, HEAD_DIM), 0.1),
        "lambda_k2": nrm(ks[9], (nA, HEAD_DIM), 0.1),
        "subln_gain": 1.0 + nrm(ks[10], (nA, 2 * HEAD_DIM), 0.05),
        "w_o": nrm(ks[11], (nA, ATTN_WIDTH, D_MODEL), ATTN_WIDTH ** -0.5),
        "pool_norm": 1.0 + nrm(ks[12], (nP, D_MODEL), 0.05),
        "pool_w_in": nrm(ks[13], (nP, D_MODEL, D_MODEL), D_MODEL ** -0.5),
        "pool_w_group": nrm(ks[14], (nP, N_POOL_GROUPS, GROUP_WIDTH, GROUP_WIDTH), GROUP_WIDTH ** -0.5),
        "pool_w_out": nrm(ks[15], (nP, D_MODEL, D_MODEL), D_MODEL ** -0.5),
        "pool_scale": 1.0 + nrm(ks[16], (nP, D_MODEL), 0.1),
        "ffn_norm": 1.0 + nrm(ks[17], (L, D_MODEL), 0.05),
        "router_group_w": nrm(ks[18], (L, D_MODEL, N_EXPERT_GROUPS), D_MODEL ** -0.5),
        "router_group_b": nrm(ks[19], (L, N_EXPERT_GROUPS), 0.01),
        "router_expert_w": nrm(ks[20], (L, D_MODEL, N_EXPERTS), D_MODEL ** -0.5),
        "router_expert_b": nrm(ks[21], (L, N_EXPERTS), 0.01),
        "w_gate": nrm(ks[22], (L, N_EXPERTS, D_MODEL, EXPERT_FF), D_MODEL ** -0.5),
        "w_up": nrm(ks[23], (L, N_EXPERTS, D_MODEL, EXPERT_FF), D_MODEL ** -0.5),
        "w_down": nrm(ks[24], (L, N_EXPERTS, EXPERT_FF, D_MODEL), EXPERT_FF ** -0.5),
    }


def reference(x, rel_bias, attn_norm, w_qkv, q_gain, k_gain, lambda_q1, lambda_k1, lambda_q2, lambda_k2,
              subln_gain, w_o, pool_norm, pool_w_in, pool_w_group, pool_w_out, pool_scale,
              ffn_norm, router_group_w, router_group_b, router_expert_w, router_expert_b,
              w_gate, w_up, w_down):
    for i in range(DEPTH):
        j = i // N_MIXERS
        if i % N_MIXERS == 0:
            lambda_init = 0.8 - 0.6 * math.exp(-0.3 * i)
            h = rms_norm(x, attn_norm[j])
            x = x + diff_attention(h, w_qkv[j], q_gain[j], k_gain[j], lambda_q1[j], lambda_k1[j],
                                   lambda_q2[j], lambda_k2[j], subln_gain[j], w_o[j], rel_bias, lambda_init)
        else:
            h = rms_norm(x, pool_norm[j])
            x = x + multiscale_pool(h, pool_w_in[j], pool_w_group[j], pool_w_out[j], pool_scale[j])
        h = rms_norm(x, ffn_norm[i])
        x = x + hier_moe(h, router_group_w[i], router_group_b[i], router_expert_w[i], router_expert_b[i],
                         w_gate[i], w_up[i], w_down[i])
    return x
```

```python
import functools
import math

import jax
import jax.numpy as jnp
from jax import lax
from jax.experimental import pallas as pl
from jax.experimental.pallas import tpu as pltpu

F32 = jnp.float32
BF16 = jnp.bfloat16
U32 = jnp.uint32
I32 = jnp.int32

EPS = 1e-6
HEAD_DIM = 64
HEAD_W = 2 * HEAD_DIM
NUM_BUCKETS = 32
MAX_EXACT = NUM_BUCKETS // 2
MAX_DISTANCE = 128
POOL_WINDOWS = (2, 4, 8, 16)
N_EXPERT_GROUPS = 4
EXPERTS_PER_GROUP = 8
N_EXPERTS = N_EXPERT_GROUPS * EXPERTS_PER_GROUP
TOP_K = 2
N_MIXERS = 2

LANES = 128
VMEM_LIMIT = 48 * 1024 * 1024
ROW_TILE = 512
ATTN_TILE = 256
EXPERT_BLOCK = 256
MOVE_TILE = 256
POOL_HALO = 16
MASK_VALUE = -1e30
ROUTER_COLS = LANES


def _params(*sem):
    return pltpu.CompilerParams(dimension_semantics=sem, vmem_limit_bytes=VMEM_LIMIT)


def _rms(x, gain):
    ms = jnp.mean(x * x, axis=-1, keepdims=True)
    return x * lax.rsqrt(ms + EPS) * gain


def _pack_rows(v):
    c = v.shape[1] // 2
    bits = lax.bitcast_convert_type(v.astype(BF16).astype(F32), U32)
    return (bits[:, :c] >> 16) | (bits[:, c:] & jnp.uint32(0xFFFF0000))


def _unpack_rows(w):
    lo = lax.bitcast_convert_type(w << 16, F32)
    hi = lax.bitcast_convert_type(w & jnp.uint32(0xFFFF0000), F32)
    return jnp.concatenate([lo, hi], axis=1)


def _norm_matmul_kernel(x_ref, g_ref, w_ref, o_ref, *, n_chunks):
    h = _rms(x_ref[...], g_ref[...]).astype(BF16)
    cw = o_ref.shape[1] // n_chunks
    for c in range(n_chunks):
        o_ref[:, c * cw:(c + 1) * cw] = jnp.dot(
            h, w_ref[:, c * cw:(c + 1) * cw], preferred_element_type=F32).astype(o_ref.dtype)


def _norm_matmul(x2, gain, w):
    t, d = x2.shape
    n = w.shape[1]
    tm = min(ROW_TILE, t)
    return pl.pallas_call(
        functools.partial(_norm_matmul_kernel, n_chunks=n // d),
        grid=(t // tm,),
        in_specs=[pl.BlockSpec((tm, d), lambda i: (i, 0)),
                  pl.BlockSpec((1, d), lambda i: (0, 0)),
                  pl.BlockSpec((d, n), lambda i: (0, 0))],
        out_specs=pl.BlockSpec((tm, n), lambda i: (i, 0)),
        out_shape=jax.ShapeDtypeStruct((t, n), BF16),
        compiler_params=_params("parallel"),
        name="norm_matmul",
    )(x2, gain, w)


def _matmul_residual_kernel(a_ref, w_ref, x_ref, o_ref):
    o_ref[...] = x_ref[...] + jnp.dot(a_ref[...], w_ref[...], preferred_element_type=F32)


def _matmul_residual(a, w, x2):
    t, d = x2.shape
    k = a.shape[1]
    tm = min(ROW_TILE, t)
    return pl.pallas_call(
        _matmul_residual_kernel,
        grid=(t // tm,),
        in_specs=[pl.BlockSpec((tm, k), lambda i: (i, 0)),
                  pl.BlockSpec((k, d), lambda i: (0, 0)),
                  pl.BlockSpec((tm, d), lambda i: (i, 0))],
        out_specs=pl.BlockSpec((tm, d), lambda i: (i, 0)),
        out_shape=jax.ShapeDtypeStruct((t, d), F32),
        compiler_params=_params("parallel"),
        name="matmul_residual",
    )(a, w, x2)


def _rel_bucket(dist):
    n = jnp.maximum(dist, 0)
    nf = jnp.maximum(n, 1).astype(F32)
    large = MAX_EXACT + (jnp.log(nf / MAX_EXACT) / math.log(MAX_DISTANCE / MAX_EXACT)
                         * (NUM_BUCKETS - MAX_EXACT)).astype(I32)
    large = jnp.minimum(large, NUM_BUCKETS - 1)
    return jnp.where(n < MAX_EXACT, n, large)


def _bias_tiles(rel_bias):
    assert ATTN_TILE >= MAX_DISTANCE
    i = jnp.arange(ATTN_TILE)[:, None]
    j = jnp.arange(ATTN_TILE)[None, :]
    table = rel_bias.astype(F32)
    far = table[NUM_BUCKETS - 1]
    tiles = []
    for off in (0, ATTN_TILE):
        dist = i - j + off
        b = (table[_rel_bucket(dist)] - far).transpose(2, 0, 1)
        tiles.append(jnp.where((dist >= 0)[None], b, MASK_VALUE))
    return jnp.stack(tiles, axis=1)


def _attn_kernel(q_ref, k_ref, v_ref, bias_ref, qg_ref, kg_ref, lam_ref, sg_ref, o_ref,
                 kn_ref, acc_ref, m_ref, l_ref, *, lambda_init):
    seq = k_ref.shape[1]
    tq = ATTN_TILE
    first = lax.broadcasted_iota(I32, (1, HEAD_W), 1) < HEAD_DIM

    def half_norm(x, gain):
        sq = x * x
        s1 = jnp.sum(jnp.where(first, sq, 0.0), axis=-1, keepdims=True)
        s2 = jnp.sum(jnp.where(first, 0.0, sq), axis=-1, keepdims=True)
        r = jnp.where(first, lax.rsqrt(s1 / HEAD_DIM + EPS), lax.rsqrt(s2 / HEAD_DIM + EPS))
        return x * r * gain

    def norm_keys(c, carry):
        r0 = pl.multiple_of(c * tq, tq)
        k = k_ref[0, pl.ds(r0, tq), :].astype(F32)
        kn_ref[pl.ds(r0, tq), :] = half_norm(k, kg_ref[...]).astype(BF16)
        return carry

    lax.fori_loop(0, seq // tq, norm_keys, 0)

    lp = lam_ref[...]
    lam = (jnp.exp(jnp.sum(lp[0:1] * lp[1:2], axis=-1, keepdims=True))
           - jnp.exp(jnp.sum(lp[2:3] * lp[3:4], axis=-1, keepdims=True)) + lambda_init)

    def query_tile(qi, carry):
        q0 = pl.multiple_of(qi * tq, tq)
        qn = half_norm(q_ref[0, pl.ds(q0, tq), :].astype(F32), qg_ref[...]) * (HEAD_DIM ** -0.5)
        qp = jnp.concatenate([jnp.where(first, qn, 0.0), jnp.where(first, 0.0, qn)], axis=0).astype(BF16)
        m_ref[...] = jnp.full(m_ref.shape, MASK_VALUE, F32)
        l_ref[...] = jnp.zeros(l_ref.shape, F32)
        acc_ref[...] = jnp.zeros(acc_ref.shape, F32)

        def kv_step(kj, bias):
            k0 = pl.multiple_of(kj * tq, tq)
            s = lax.dot_general(qp, kn_ref[pl.ds(k0, tq), :], (((1,), (1,)), ((), ())),
                                preferred_element_type=F32)
            if bias is not None:
                s = s + jnp.concatenate([bias, bias], axis=0)
            m_old = m_ref[...]
            m_new = jnp.maximum(m_old, jnp.max(s, axis=-1, keepdims=True))
            alpha = jnp.exp(m_old - m_new)
            p = jnp.exp(s - m_new)
            l_ref[...] = alpha * l_ref[...] + jnp.sum(p, axis=-1, keepdims=True)
            acc_ref[...] = alpha * acc_ref[...] + jnp.dot(
                p.astype(BF16), v_ref[0, pl.ds(k0, tq), :], preferred_element_type=F32)
            m_ref[...] = m_new

        def far_step(kj, c):
            kv_step(kj, None)
            return c

        lax.fori_loop(0, jnp.maximum(qi - 1, 0), far_step, 0)

        @pl.when(qi >= 1)
        def _():
            kv_step(qi - 1, bias_ref[0, 1])

        kv_step(qi, bias_ref[0, 0])

        o_all = acc_ref[...] / l_ref[...]
        o = o_all[:tq] - lam * o_all[tq:]
        o = _rms(o, sg_ref[...]) * (1.0 - lambda_init)
        o_ref[0, pl.ds(q0, tq), :] = o.astype(o_ref.dtype)
        return carry

    lax.fori_loop(0, seq // tq, query_tile, 0)


def _attention(qkv, bias_tiles, q_gain2, k_gain2, lam_params, subln_gain, lambda_init):
    b, s, w3 = qkv.shape
    w = w3 // 3
    n_heads = w // HEAD_W
    assert s % ATTN_TILE == 0
    tq = ATTN_TILE
    return pl.pallas_call(
        functools.partial(_attn_kernel, lambda_init=lambda_init),
        grid=(b, n_heads),
        in_specs=[pl.BlockSpec((1, s, HEAD_W), lambda bi, h: (bi, 0, h)),
                  pl.BlockSpec((1, s, HEAD_W), lambda bi, h: (bi, 0, n_heads + h)),
                  pl.BlockSpec((1, s, HEAD_W), lambda bi, h: (bi, 0, 2 * n_heads + h)),
                  pl.BlockSpec((1, 2, tq, tq), lambda bi, h: (h, 0, 0, 0)),
                  pl.BlockSpec((1, HEAD_W), lambda bi, h: (0, 0)),
                  pl.BlockSpec((1, HEAD_W), lambda bi, h: (0, 0)),
                  pl.BlockSpec((4, HEAD_DIM), lambda bi, h: (0, 0)),
                  pl.BlockSpec((1, HEAD_W), lambda bi, h: (0, 0))],
        out_specs=pl.BlockSpec((1, s, HEAD_W), lambda bi, h: (bi, 0, h)),
        out_shape=jax.ShapeDtypeStruct((b, s, w), BF16),
        scratch_shapes=[pltpu.VMEM((s, HEAD_W), BF16),
                        pltpu.VMEM((2 * tq, HEAD_W), F32),
                        pltpu.VMEM((2 * tq, 1), F32),
                        pltpu.VMEM((2 * tq, 1), F32)],
        compiler_params=_params("parallel", "parallel"),
        name="diff_attention",
    )(qkv, qkv, qkv, bias_tiles, q_gain2, k_gain2, lam_params, subln_gain)


def _pool_kernel(x_ref, g_ref, win_ref, wgrp_ref, wout_ref, scale_ref, o_ref, ext_ref, pooled_ref):
    j = pl.program_id(1)
    tm = x_ref.shape[1]
    d = x_ref.shape[2]
    gw = d // len(POOL_WINDOWS)
    x = x_ref[0]

    @pl.when(j == 0)
    def _():
        ext_ref[0:POOL_HALO, :] = jnp.zeros((POOL_HALO, d), F32)

    @pl.when(j > 0)
    def _():
        ext_ref[0:POOL_HALO, :] = ext_ref[tm:tm + POOL_HALO, :]

    h = _rms(x, g_ref[...]).astype(BF16)
    ext_ref[POOL_HALO:POOL_HALO + tm, :] = jnp.dot(h, win_ref[...], preferred_element_type=F32)

    pos1 = (j * tm + 1 + lax.broadcasted_iota(I32, (tm, 1), 0)).astype(F32)
    for g, win in enumerate(POOL_WINDOWS):
        c0, c1 = g * gw, (g + 1) * gw
        u = ext_ref[POOL_HALO:POOL_HALO + tm, c0:c1]
        s = u
        for k in range(1, win):
            s = s + ext_ref[POOL_HALO - k:POOL_HALO - k + tm, c0:c1]
        inv_cnt = 1.0 / jnp.minimum(pos1, float(win))
        pooled = (s * inv_cnt - u).astype(BF16)
        pooled_ref[:, c0:c1] = jnp.dot(pooled, wgrp_ref[g], preferred_element_type=F32).astype(BF16)

    y = jnp.dot(pooled_ref[...], wout_ref[...], preferred_element_type=F32)
    o_ref[0] = x + y * scale_ref[...]


def _pool_mixer(x, gain, w_in, w_group, w_out, scale):
    b, s, d = x.shape
    tm = min(ROW_TILE, s)
    ng = len(POOL_WINDOWS)
    gw = d // ng
    return pl.pallas_call(
        _pool_kernel,
        grid=(b, s // tm),
        in_specs=[pl.BlockSpec((1, tm, d), lambda bi, j: (bi, j, 0)),
                  pl.BlockSpec((1, d), lambda bi, j: (0, 0)),
                  pl.BlockSpec((d, d), lambda bi, j: (0, 0)),
                  pl.BlockSpec((ng, gw, gw), lambda bi, j: (0, 0, 0)),
                  pl.BlockSpec((d, d), lambda bi, j: (0, 0)),
                  pl.BlockSpec((1, d), lambda bi, j: (0, 0))],
        out_specs=pl.BlockSpec((1, tm, d), lambda bi, j: (bi, j, 0)),
        out_shape=jax.ShapeDtypeStruct((b, s, d), F32),
        scratch_shapes=[pltpu.VMEM((POOL_HALO + tm, d), F32),
                        pltpu.VMEM((tm, d), BF16)],
        compiler_params=_params("arbitrary", "arbitrary"),
        name="pool_mixer",
    )(x, gain, w_in, w_group, w_out, scale)


def _router_kernel(x_ref, g_ref, wr_ref, br_ref, o_ref):
    tm = x_ref.shape[0]
    h = _rms(x_ref[...], g_ref[...])
    logits = jnp.dot(h, wr_ref[...], preferred_element_type=F32,
                     precision=lax.Precision.HIGHEST) + br_ref[...]
    lt = logits.T

    def row(r):
        return lt[r:r + 1, :]

    g_best = row(N_EXPERTS)
    g_idx = jnp.zeros((1, tm), I32)
    for g in range(1, N_EXPERT_GROUPS):
        v = row(N_EXPERTS + g)
        better = v > g_best
        g_best = jnp.where(better, v, g_best)
        g_idx = jnp.where(better, g, g_idx)
    denom = jnp.zeros((1, tm), F32)
    for g in range(N_EXPERT_GROUPS):
        denom = denom + jnp.exp(row(N_EXPERTS + g) - g_best)
    g_prob = 1.0 / denom

    sel = []
    for e in range(EXPERTS_PER_GROUP):
        v = row(e)
        for g in range(1, N_EXPERT_GROUPS):
            v = jnp.where(g_idx == g, row(g * EXPERTS_PER_GROUP + e), v)
        sel.append(v)

    def top1(vals):
        best, idx = vals[0], jnp.zeros((1, tm), I32)
        for e in range(1, EXPERTS_PER_GROUP):
            better = vals[e] > best
            best = jnp.where(better, vals[e], best)
            idx = jnp.where(better, e, idx)
        return best, idx

    v0, i0 = top1(sel)
    v1, i1 = top1([jnp.where(i0 == e, -jnp.inf, sel[e]) for e in range(EXPERTS_PER_GROUP)])
    t = jnp.exp(v1 - v0)
    w0 = g_prob / (1.0 + t)
    w1 = w0 * t
    base = g_idx * EXPERTS_PER_GROUP
    o_ref[...] = jnp.concatenate(
        [(base + i0).astype(F32), (base + i1).astype(F32), w0, w1, jnp.zeros((4, tm), F32)], axis=0)


def _router(x2, gain, wr, br):
    t, d = x2.shape
    tm = min(ROW_TILE, t)
    return pl.pallas_call(
        _router_kernel,
        grid=(t // tm,),
        in_specs=[pl.BlockSpec((tm, d), lambda i: (i, 0)),
                  pl.BlockSpec((1, d), lambda i: (0, 0)),
                  pl.BlockSpec((d, ROUTER_COLS), lambda i: (0, 0)),
                  pl.BlockSpec((1, ROUTER_COLS), lambda i: (0, 0))],
        out_specs=pl.BlockSpec((8, tm), lambda i: (0, i)),
        out_shape=jax.ShapeDtypeStruct((8, t), F32),
        compiler_params=_params("parallel"),
        name="router",
    )(x2, gain, wr, br)


def _slot_positions(route, blk, n_blocks):
    e = route[0:TOP_K].astype(I32).T.reshape(-1)
    onehot = (e[:, None] == jnp.arange(N_EXPERTS, dtype=I32)[None, :]).astype(I32)
    csum = jnp.cumsum(onehot, axis=0)
    rank = jnp.sum(onehot * csum, axis=1) - 1
    counts = csum[-1]
    padded = (counts + blk - 1) // blk * blk
    pad_end = jnp.cumsum(padded)
    pos = (pad_end - padded)[e] + rank
    block_expert = jnp.minimum(
        jnp.searchsorted(pad_end, jnp.arange(n_blocks, dtype=I32) * blk, side="right"), N_EXPERTS - 1)
    n_used = (pad_end[-1] // blk).reshape(1)
    return pos.astype(I32), block_expert.astype(I32), n_used.astype(I32)


def _row_copy(src_ref, src_row, dst_ref, dst_row, sem):
    return pltpu.make_async_copy(src_ref.at[pl.ds(src_row, 1)], dst_ref.at[pl.ds(dst_row, 1)], sem)


def _dispatch_kernel(pos_ref, x_ref, g_ref, xs_in_ref, xs_ref, buf_ref, sem):
    del xs_in_ref
    i = pl.program_id(0)
    n = pl.num_programs(0)
    tm = x_ref.shape[0]
    slot = i % 2
    n_copies = TOP_K * tm

    def wait_slot(sl):
        def body(r, c):
            _row_copy(buf_ref.at[sl], 0, xs_ref, 0, sem.at[sl]).wait()
            return c
        lax.fori_loop(0, n_copies, body, 0)

    @pl.when(i >= 2)
    def _():
        wait_slot(slot)

    buf_ref[slot] = _pack_rows(_rms(x_ref[...], g_ref[...]))

    def issue(r, c):
        for k in range(TOP_K):
            p = pos_ref[(i * tm + r) * TOP_K + k]
            _row_copy(buf_ref.at[slot], r, xs_ref, p, sem.at[slot]).start()
        return c

    lax.fori_loop(0, tm, issue, 0)

    @pl.when(i == n - 1)
    def _():
        wait_slot(slot)

    @pl.when((i == n - 1) & (n >= 2))
    def _():
        wait_slot(1 - slot)


def _dispatch(pos, x2, gain, n_slots):
    t, d = x2.shape
    tm = min(MOVE_TILE, t)
    zeros = jnp.zeros((n_slots, d // 2), U32)
    grid_spec = pltpu.PrefetchScalarGridSpec(
        num_scalar_prefetch=1,
        grid=(t // tm,),
        in_specs=[pl.BlockSpec((tm, d), lambda i, pos: (i, 0)),
                  pl.BlockSpec((1, d), lambda i, pos: (0, 0)),
                  pl.BlockSpec(memory_space=pl.ANY)],
        out_specs=pl.BlockSpec(memory_space=pl.ANY),
        scratch_shapes=[pltpu.VMEM((2, tm, d // 2), U32),
                        pltpu.SemaphoreType.DMA((2,))],
    )
    return pl.pallas_call(
        _dispatch_kernel,
        grid_spec=grid_spec,
        out_shape=jax.ShapeDtypeStruct((n_slots, d // 2), U32),
        input_output_aliases={3: 0},
        compiler_params=_params("arbitrary"),
        name="dispatch",
    )(pos, x2, gain, zeros)


def _expert_kernel(be_ref, nu_ref, xs_ref, wg_ref, wu_ref, wd_ref, y_ref, wgb_ref, wub_ref, wdb_ref):
    i = pl.program_id(0)
    e = be_ref[i]
    prev = be_ref[jnp.maximum(i - 1, 0)]

    @pl.when((i == 0) | (e != prev))
    def _():
        wgb_ref[...] = wg_ref[0, 0].astype(BF16)
        wub_ref[...] = wu_ref[0, 0].astype(BF16)
        wdb_ref[...] = wd_ref[0, 0].astype(BF16)

    @pl.when(i < nu_ref[0])
    def _():
        x = _unpack_rows(xs_ref[...]).astype(BF16)
        g = jnp.dot(x, wgb_ref[...], preferred_element_type=F32)
        u = jnp.dot(x, wub_ref[...], preferred_element_type=F32)
        a = (g / (1.0 + jnp.exp(-g)) * u).astype(BF16)
        y_ref[...] = _pack_rows(jnp.dot(a, wdb_ref[...], preferred_element_type=F32))

    @pl.when(i >= nu_ref[0])
    def _():
        y_ref[...] = jnp.zeros(y_ref.shape, U32)


def _experts(block_expert, n_used, xs, w_gate, w_up, w_down, layer):
    n_slots, dh = xs.shape
    blk = EXPERT_BLOCK
    d, ff = w_gate.shape[2], w_gate.shape[3]
    grid_spec = pltpu.PrefetchScalarGridSpec(
        num_scalar_prefetch=2,
        grid=(n_slots // blk,),
        in_specs=[pl.BlockSpec((blk, dh), lambda i, be, nu: (i, 0)),
                  pl.BlockSpec((1, 1, d, ff), lambda i, be, nu: (layer, be[i], 0, 0)),
                  pl.BlockSpec((1, 1, d, ff), lambda i, be, nu: (layer, be[i], 0, 0)),
                  pl.BlockSpec((1, 1, ff, d), lambda i, be, nu: (layer, be[i], 0, 0))],
        out_specs=pl.BlockSpec((blk, dh), lambda i, be, nu: (i, 0)),
        scratch_shapes=[pltpu.VMEM((d, ff), BF16),
                        pltpu.VMEM((d, ff), BF16),
                        pltpu.VMEM((ff, d), BF16)],
    )
    return pl.pallas_call(
        _expert_kernel,
        grid_spec=grid_spec,
        out_shape=jax.ShapeDtypeStruct((n_slots, dh), U32),
        compiler_params=_params("arbitrary"),
        name="experts",
    )(block_expert, n_used, xs, w_gate, w_up, w_down)


def _combine_kernel(pos_ref, y_ref, x_ref, w_ref, o_ref, buf_ref, sem):
    i = pl.program_id(0)
    n = pl.num_programs(0)
    tm = x_ref.shape[0]
    slot = i % 2

    def issue_tile(tile, sl):
        def body(r, c):
            for k in range(TOP_K):
                p = pos_ref[(tile * tm + r) * TOP_K + k]
                _row_copy(y_ref, p, buf_ref.at[sl], k * tm + r, sem.at[sl]).start()
            return c
        lax.fori_loop(0, tm, body, 0)

    @pl.when(i == 0)
    def _():
        issue_tile(0, 0)

    @pl.when(i + 1 < n)
    def _():
        issue_tile(i + 1, 1 - slot)

    def wait_body(r, c):
        _row_copy(y_ref, 0, buf_ref.at[slot], 0, sem.at[slot]).wait()
        return c

    lax.fori_loop(0, TOP_K * tm, wait_body, 0)

    rows = buf_ref[slot]
    w = w_ref[...]
    out = x_ref[...]
    for k in range(TOP_K):
        out = out + w[:, k:k + 1] * _unpack_rows(rows[k * tm:(k + 1) * tm])
    o_ref[...] = out


def _combine(pos, y, x2, weights):
    t, d = x2.shape
    tm = min(MOVE_TILE, t)
    grid_spec = pltpu.PrefetchScalarGridSpec(
        num_scalar_prefetch=1,
        grid=(t // tm,),
        in_specs=[pl.BlockSpec(memory_space=pl.ANY),
                  pl.BlockSpec((tm, d), lambda i, pos: (i, 0)),
                  pl.BlockSpec((tm, TOP_K), lambda i, pos: (i, 0))],
        out_specs=pl.BlockSpec((tm, d), lambda i, pos: (i, 0)),
        scratch_shapes=[pltpu.VMEM((2, TOP_K * tm, d // 2), U32),
                        pltpu.SemaphoreType.DMA((2,))],
    )
    return pl.pallas_call(
        _combine_kernel,
        grid_spec=grid_spec,
        out_shape=jax.ShapeDtypeStruct((t, d), F32),
        compiler_params=_params("arbitrary"),
        name="combine",
    )(pos, y, x2, weights)


def _hier_moe(x2, gain, wg1, bg1, wg2, bg2, w_gate, w_up, w_down, layer):
    t, d = x2.shape
    wr = jnp.zeros((d, ROUTER_COLS), F32)
    wr = wr.at[:, :N_EXPERTS].set(wg2.astype(F32)).at[:, N_EXPERTS:N_EXPERTS + N_EXPERT_GROUPS].set(wg1.astype(F32))
    br = jnp.zeros((1, ROUTER_COLS), F32)
    br = br.at[0, :N_EXPERTS].set(bg2.astype(F32)).at[0, N_EXPERTS:N_EXPERTS + N_EXPERT_GROUPS].set(bg1.astype(F32))
    route = _router(x2, gain, wr, br)

    blk = EXPERT_BLOCK
    n_blocks = -(-(t * TOP_K) // blk) + N_EXPERTS
    pos, block_expert, n_used = _slot_positions(route, blk, n_blocks)
    xs = _dispatch(pos, x2, gain, n_blocks * blk)
    y = _experts(block_expert, n_used, xs, w_gate, w_up, w_down, layer)
    return _combine(pos, y, x2, route[TOP_K:2 * TOP_K].T)


def kernel(x, rel_bias, attn_norm, w_qkv, q_gain, k_gain, lambda_q1, lambda_k1, lambda_q2, lambda_k2,
           subln_gain, w_o, pool_norm, pool_w_in, pool_w_group, pool_w_out, pool_scale,
           ffn_norm, router_group_w, router_group_b, router_expert_w, router_expert_b,
           w_gate, w_up, w_down):
    b, s, d = x.shape
    depth = ffn_norm.shape[0]
    x = x.astype(F32)
    bias_tiles = _bias_tiles(rel_bias)
    for i in range(depth):
        j = i // N_MIXERS
        if i % N_MIXERS == 0:
            lambda_init = 0.8 - 0.6 * math.exp(-0.3 * i)
            x2 = x.reshape(b * s, d)
            qkv = _norm_matmul(x2, attn_norm[j].reshape(1, d).astype(F32), w_qkv[j].astype(BF16))
            lam_params = jnp.stack([lambda_q1[j], lambda_k1[j], lambda_q2[j], lambda_k2[j]]).astype(F32)
            o = _attention(qkv.reshape(b, s, -1), bias_tiles,
                           jnp.tile(q_gain[j].astype(F32), 2).reshape(1, HEAD_W),
                           jnp.tile(k_gain[j].astype(F32), 2).reshape(1, HEAD_W),
                           lam_params, subln_gain[j].reshape(1, HEAD_W).astype(F32), lambda_init)
            x2 = _matmul_residual(o.reshape(b * s, -1), w_o[j].astype(BF16), x2)
        else:
            x3 = _pool_mixer(x, pool_norm[j].reshape(1, d).astype(F32), pool_w_in[j].astype(BF16),
                             pool_w_group[j].astype(BF16), pool_w_out[j].astype(BF16),
                             pool_scale[j].reshape(1, d).astype(F32))
            x2 = x3.reshape(b * s, d)
        x2 = _hier_moe(x2, ffn_norm[i].reshape(1, d).astype(F32), router_group_w[i], router_group_b[i],
                       router_expert_w[i], router_expert_b[i], w_gate, w_up, w_down, i)
        x = x2.reshape(b, s, d)
    return x
```

```python
import functools
import math

import jax
import jax.numpy as jnp
from jax import lax
from jax.experimental import pallas as pl
from jax.experimental.pallas import tpu as pltpu

F32 = jnp.float32
BF16 = jnp.bfloat16
U32 = jnp.uint32
I32 = jnp.int32

EPS = 1e-6
HEAD_DIM = 64
HEAD_W = 2 * HEAD_DIM
NUM_BUCKETS = 32
MAX_EXACT = NUM_BUCKETS // 2
MAX_DISTANCE = 128
POOL_WINDOWS = (2, 4, 8, 16)
N_EXPERT_GROUPS = 4
EXPERTS_PER_GROUP = 8
N_EXPERTS = N_EXPERT_GROUPS * EXPERTS_PER_GROUP
TOP_K = 2
N_MIXERS = 2
LOG2E = math.log2(math.e)

LANES = 128
VMEM_LIMIT = 48 * 1024 * 1024
ROW_TILE = 512
ATTN_TILE = 512
ATTN_PREP_ROWS = 256
EXPERT_BLOCK = 256
MOVE_TILE = 256
ISSUE_UNROLL = 8
POOL_HALO = 16
MASK_VALUE = -1e30
ROUTER_COLS = LANES


def _params(*sem):
    return pltpu.CompilerParams(dimension_semantics=sem, vmem_limit_bytes=VMEM_LIMIT)


def _rms(x, gain):
    ms = jnp.mean(x * x, axis=-1, keepdims=True)
    return x * lax.rsqrt(ms + EPS) * gain


def _pack_rows(v):
    c = v.shape[1] // 2
    bits = lax.bitcast_convert_type(v.astype(BF16).astype(F32), U32)
    return (bits[:, :c] >> 16) | (bits[:, c:] & jnp.uint32(0xFFFF0000))


def _unpack_rows(w):
    lo = lax.bitcast_convert_type(w << 16, F32)
    hi = lax.bitcast_convert_type(w & jnp.uint32(0xFFFF0000), F32)
    return jnp.concatenate([lo, hi], axis=1)


def _norm_matmul_kernel(x_ref, g_ref, w_ref, o_ref, *, n_chunks):
    h = _rms(x_ref[...], g_ref[...]).astype(BF16)
    cw = o_ref.shape[1] // n_chunks
    for c in range(n_chunks):
        o_ref[:, c * cw:(c + 1) * cw] = jnp.dot(
            h, w_ref[:, c * cw:(c + 1) * cw], preferred_element_type=F32).astype(o_ref.dtype)


def _norm_matmul(x2, gain, w):
    t, d = x2.shape
    n = w.shape[1]
    tm = min(ROW_TILE, t)
    return pl.pallas_call(
        functools.partial(_norm_matmul_kernel, n_chunks=n // d),
        grid=(t // tm,),
        in_specs=[pl.BlockSpec((tm, d), lambda i: (i, 0)),
                  pl.BlockSpec((1, d), lambda i: (0, 0)),
                  pl.BlockSpec((d, n), lambda i: (0, 0))],
        out_specs=pl.BlockSpec((tm, n), lambda i: (i, 0)),
        out_shape=jax.ShapeDtypeStruct((t, n), BF16),
        compiler_params=_params("parallel"),
        name="norm_matmul",
    )(x2, gain, w)


def _matmul_residual_kernel(a_ref, w_ref, x_ref, o_ref):
    o_ref[...] = x_ref[...] + jnp.dot(a_ref[...], w_ref[...], preferred_element_type=F32)


def _matmul_residual(a, w, x2):
    t, d = x2.shape
    k = a.shape[1]
    tm = min(ROW_TILE, t)
    return pl.pallas_call(
        _matmul_residual_kernel,
        grid=(t // tm,),
        in_specs=[pl.BlockSpec((tm, k), lambda i: (i, 0)),
                  pl.BlockSpec((k, d), lambda i: (0, 0)),
                  pl.BlockSpec((tm, d), lambda i: (i, 0))],
        out_specs=pl.BlockSpec((tm, d), lambda i: (i, 0)),
        out_shape=jax.ShapeDtypeStruct((t, d), F32),
        compiler_params=_params("parallel"),
        name="matmul_residual",
    )(a, w, x2)


def _rel_bucket(dist):
    n = jnp.maximum(dist, 0)
    nf = jnp.maximum(n, 1).astype(F32)
    large = MAX_EXACT + (jnp.log(nf / MAX_EXACT) / math.log(MAX_DISTANCE / MAX_EXACT)
                         * (NUM_BUCKETS - MAX_EXACT)).astype(I32)
    large = jnp.minimum(large, NUM_BUCKETS - 1)
    return jnp.where(n < MAX_EXACT, n, large)


def _bias_tiles(rel_bias):
    assert ATTN_TILE >= MAX_DISTANCE
    kj = jnp.arange(ATTN_TILE)[:, None]
    qi = jnp.arange(ATTN_TILE)[None, :]
    table = rel_bias.astype(F32)
    far = table[NUM_BUCKETS - 1]
    tiles = []
    for off in (0, ATTN_TILE):
        dist = qi - kj + off
        b = ((table[_rel_bucket(dist)] - far) * LOG2E).transpose(2, 0, 1)
        tiles.append(jnp.where((dist >= 0)[None], b, MASK_VALUE))
    return jnp.stack(tiles, axis=1)


def _attn_kernel(q_ref, k_ref, v_ref, bias_ref, qg_ref, kg_ref, lam_ref, sg_ref, o_ref,
                 kn_ref, vt_ref, qpt_ref, acc_ref, m_ref, l_ref, *, lambda_init):
    seq = k_ref.shape[1]
    tq = ATTN_TILE
    pr = ATTN_PREP_ROWS
    first = lax.broadcasted_iota(I32, (1, HEAD_W), 1) < HEAD_DIM

    def half_norm(x, gain):
        sq = x * x
        s1 = jnp.sum(jnp.where(first, sq, 0.0), axis=-1, keepdims=True)
        s2 = jnp.sum(jnp.where(first, 0.0, sq), axis=-1, keepdims=True)
        r = jnp.where(first, lax.rsqrt(s1 / HEAD_DIM + EPS), lax.rsqrt(s2 / HEAD_DIM + EPS))
        return x * r * gain

    def prep(c, carry):
        r0 = pl.multiple_of(c * pr, pr)
        k = k_ref[0, pl.ds(r0, pr), :].astype(F32)
        kn_ref[pl.ds(r0, pr), :] = half_norm(k, kg_ref[...]).astype(BF16)
        vt_ref[:, pl.ds(r0, pr)] = v_ref[0, pl.ds(r0, pr), :].astype(F32).T.astype(BF16)
        return carry

    lax.fori_loop(0, seq // pr, prep, 0)

    lp = lam_ref[...]
    lam = (jnp.exp(jnp.sum(lp[0:1] * lp[1:2], axis=-1, keepdims=True))
           - jnp.exp(jnp.sum(lp[2:3] * lp[3:4], axis=-1, keepdims=True)) + lambda_init)
    first_rows = lax.broadcasted_iota(I32, (HEAD_W, 1), 0) < HEAD_DIM

    def query_tile(qi, carry):
        q0 = pl.multiple_of(qi * tq, tq)
        qn = half_norm(q_ref[0, pl.ds(q0, tq), :].astype(F32), qg_ref[...]) * (HEAD_DIM ** -0.5 * LOG2E)
        qnt = qn.T
        qpt_ref[:, 0:tq] = jnp.where(first_rows, qnt, 0.0).astype(BF16)
        qpt_ref[:, tq:2 * tq] = jnp.where(first_rows, 0.0, qnt).astype(BF16)
        m_ref[...] = jnp.full(m_ref.shape, MASK_VALUE, F32)
        l_ref[...] = jnp.zeros(l_ref.shape, F32)
        acc_ref[...] = jnp.zeros(acc_ref.shape, F32)

        def kv_step(kj, bias):
            k0 = pl.multiple_of(kj * tq, tq)
            kt = kn_ref[pl.ds(k0, tq), :]
            vt = vt_ref[:, pl.ds(k0, tq)]
            for c in range(2):
                cols = slice(c * tq, (c + 1) * tq)
                s = jnp.dot(kt, qpt_ref[:, cols], preferred_element_type=F32)
                if bias is not None:
                    s = s + bias
                m_old = m_ref[:, cols]
                m_new = jnp.maximum(m_old, jnp.max(s, axis=0, keepdims=True))
                alpha = jnp.exp2(m_old - m_new)
                p = jnp.exp2(s - m_new)
                l_ref[:, cols] = alpha * l_ref[:, cols] + jnp.sum(p, axis=0, keepdims=True)
                acc_ref[:, cols] = alpha * acc_ref[:, cols] + jnp.dot(
                    vt, p.astype(BF16), preferred_element_type=F32)
                m_ref[:, cols] = m_new

        def far_step(kj, c):
            kv_step(kj, None)
            return c

        lax.fori_loop(0, jnp.maximum(qi - 1, 0), far_step, 0)

        @pl.when(qi >= 1)
        def _():
            kv_step(qi - 1, bias_ref[0, 1])

        kv_step(qi, bias_ref[0, 0])

        o_all = acc_ref[...] / l_ref[...]
        o = (o_all[:, :tq] - lam * o_all[:, tq:]).T
        o = _rms(o, sg_ref[...]) * (1.0 - lambda_init)
        o_ref[0, pl.ds(q0, tq), :] = o.astype(o_ref.dtype)
        return carry

    lax.fori_loop(0, seq // tq, query_tile, 0)


def _attention(qkv, bias_tiles, q_gain2, k_gain2, lam_params, subln_gain, lambda_init):
    b, s, w3 = qkv.shape
    w = w3 // 3
    n_heads = w // HEAD_W
    assert s % ATTN_TILE == 0
    tq = ATTN_TILE
    return pl.pallas_call(
        functools.partial(_attn_kernel, lambda_init=lambda_init),
        grid=(b, n_heads),
        in_specs=[pl.BlockSpec((1, s, HEAD_W), lambda bi, h: (bi, 0, h)),
                  pl.BlockSpec((1, s, HEAD_W), lambda bi, h: (bi, 0, n_heads + h)),
                  pl.BlockSpec((1, s, HEAD_W), lambda bi, h: (bi, 0, 2 * n_heads + h)),
                  pl.BlockSpec((1, 2, tq, tq), lambda bi, h: (h, 0, 0, 0)),
                  pl.BlockSpec((1, HEAD_W), lambda bi, h: (0, 0)),
                  pl.BlockSpec((1, HEAD_W), lambda bi, h: (0, 0)),
                  pl.BlockSpec((4, HEAD_DIM), lambda bi, h: (0, 0)),
                  pl.BlockSpec((1, HEAD_W), lambda bi, h: (0, 0))],
        out_specs=pl.BlockSpec((1, s, HEAD_W), lambda bi, h: (bi, 0, h)),
        out_shape=jax.ShapeDtypeStruct((b, s, w), BF16),
        scratch_shapes=[pltpu.VMEM((s, HEAD_W), BF16),
                        pltpu.VMEM((HEAD_W, s), BF16),
                        pltpu.VMEM((HEAD_W, 2 * tq), BF16),
                        pltpu.VMEM((HEAD_W, 2 * tq), F32),
                        pltpu.VMEM((1, 2 * tq), F32),
                        pltpu.VMEM((1, 2 * tq), F32)],
        compiler_params=_params("parallel", "parallel"),
        name="diff_attention",
    )(qkv, qkv, qkv, bias_tiles, q_gain2, k_gain2, lam_params, subln_gain)


def _pool_kernel(x_ref, g_ref, win_ref, wgrp_ref, wout_ref, scale_ref, o_ref, ext_ref, pooled_ref):
    j = pl.program_id(1)
    tm = x_ref.shape[1]
    d = x_ref.shape[2]
    gw = d // len(POOL_WINDOWS)
    x = x_ref[0]

    @pl.when(j == 0)
    def _():
        ext_ref[0:POOL_HALO, :] = jnp.zeros((POOL_HALO, d), F32)

    @pl.when(j > 0)
    def _():
        ext_ref[0:POOL_HALO, :] = ext_ref[tm:tm + POOL_HALO, :]

    h = _rms(x, g_ref[...]).astype(BF16)
    ext_ref[POOL_HALO:POOL_HALO + tm, :] = jnp.dot(h, win_ref[...], preferred_element_type=F32)

    pos1 = (j * tm + 1 + lax.broadcasted_iota(I32, (tm, 1), 0)).astype(F32)
    for g, win in enumerate(POOL_WINDOWS):
        c0, c1 = g * gw, (g + 1) * gw
        u = ext_ref[POOL_HALO:POOL_HALO + tm, c0:c1]
        s = u
        for k in range(1, win):
            s = s + ext_ref[POOL_HALO - k:POOL_HALO - k + tm, c0:c1]
        inv_cnt = 1.0 / jnp.minimum(pos1, float(win))
        pooled = (s * inv_cnt - u).astype(BF16)
        pooled_ref[:, c0:c1] = jnp.dot(pooled, wgrp_ref[g], preferred_element_type=F32).astype(BF16)

    y = jnp.dot(pooled_ref[...], wout_ref[...], preferred_element_type=F32)
    o_ref[0] = x + y * scale_ref[...]


def _pool_mixer(x, gain, w_in, w_group, w_out, scale):
    b, s, d = x.shape
    tm = min(ROW_TILE, s)
    ng = len(POOL_WINDOWS)
    gw = d // ng
    return pl.pallas_call(
        _pool_kernel,
        grid=(b, s // tm),
        in_specs=[pl.BlockSpec((1, tm, d), lambda bi, j: (bi, j, 0)),
                  pl.BlockSpec((1, d), lambda bi, j: (0, 0)),
                  pl.BlockSpec((d, d), lambda bi, j: (0, 0)),
                  pl.BlockSpec((ng, gw, gw), lambda bi, j: (0, 0, 0)),
                  pl.BlockSpec((d, d), lambda bi, j: (0, 0)),
                  pl.BlockSpec((1, d), lambda bi, j: (0, 0))],
        out_specs=pl.BlockSpec((1, tm, d), lambda bi, j: (bi, j, 0)),
        out_shape=jax.ShapeDtypeStruct((b, s, d), F32),
        scratch_shapes=[pltpu.VMEM((POOL_HALO + tm, d), F32),
                        pltpu.VMEM((tm, d), BF16)],
        compiler_params=_params("arbitrary", "arbitrary"),
        name="pool_mixer",
    )(x, gain, w_in, w_group, w_out, scale)


def _router_kernel(x_ref, g_ref, wr_ref, br_ref, o_ref):
    tm = x_ref.shape[0]
    h = _rms(x_ref[...], g_ref[...])
    logits = jnp.dot(h, wr_ref[...], preferred_element_type=F32,
                     precision=lax.Precision.HIGHEST) + br_ref[...]
    lt = logits.T

    def row(r):
        return lt[r:r + 1, :]

    g_best = row(N_EXPERTS)
    g_idx = jnp.zeros((1, tm), I32)
    for g in range(1, N_EXPERT_GROUPS):
        v = row(N_EXPERTS + g)
        better = v > g_best
        g_best = jnp.where(better, v, g_best)
        g_idx = jnp.where(better, g, g_idx)
    denom = jnp.zeros((1, tm), F32)
    for g in range(N_EXPERT_GROUPS):
        denom = denom + jnp.exp(row(N_EXPERTS + g) - g_best)
    g_prob = 1.0 / denom

    sel = []
    for e in range(EXPERTS_PER_GROUP):
        v = row(e)
        for g in range(1, N_EXPERT_GROUPS):
            v = jnp.where(g_idx == g, row(g * EXPERTS_PER_GROUP + e), v)
        sel.append(v)

    def top1(vals):
        best, idx = vals[0], jnp.zeros((1, tm), I32)
        for e in range(1, EXPERTS_PER_GROUP):
            better = vals[e] > best
            best = jnp.where(better, vals[e], best)
            idx = jnp.where(better, e, idx)
        return best, idx

    v0, i0 = top1(sel)
    v1, i1 = top1([jnp.where(i0 == e, -jnp.inf, sel[e]) for e in range(EXPERTS_PER_GROUP)])
    t = jnp.exp(v1 - v0)
    w0 = g_prob / (1.0 + t)
    w1 = w0 * t
    base = g_idx * EXPERTS_PER_GROUP
    o_ref[...] = jnp.concatenate(
        [(base + i0).astype(F32), (base + i1).astype(F32), w0, w1, jnp.zeros((4, tm), F32)], axis=0)


def _router(x2, gain, wr, br):
    t, d = x2.shape
    tm = min(ROW_TILE, t)
    return pl.pallas_call(
        _router_kernel,
        grid=(t // tm,),
        in_specs=[pl.BlockSpec((tm, d), lambda i: (i, 0)),
                  pl.BlockSpec((1, d), lambda i: (0, 0)),
                  pl.BlockSpec((d, ROUTER_COLS), lambda i: (0, 0)),
                  pl.BlockSpec((1, ROUTER_COLS), lambda i: (0, 0))],
        out_specs=pl.BlockSpec((8, tm), lambda i: (0, i)),
        out_shape=jax.ShapeDtypeStruct((8, t), F32),
        compiler_params=_params("parallel"),
        name="router",
    )(x2, gain, wr, br)


def _slot_positions(route, blk, n_blocks):
    e = route[0:TOP_K].astype(I32).T.reshape(-1)
    onehot = (e[:, None] == jnp.arange(N_EXPERTS, dtype=I32)[None, :]).astype(I32)
    csum = jnp.cumsum(onehot, axis=0)
    rank = jnp.sum(onehot * csum, axis=1) - 1
    counts = csum[-1]
    padded = (counts + blk - 1) // blk * blk
    pad_end = jnp.cumsum(padded)
    pos = (pad_end - padded)[e] + rank
    block_expert = jnp.minimum(
        jnp.searchsorted(pad_end, jnp.arange(n_blocks, dtype=I32) * blk, side="right"), N_EXPERTS - 1)
    n_used = (pad_end[-1] // blk).reshape(1)
    return pos.astype(I32), block_expert.astype(I32), n_used.astype(I32)


def _row_copy(src_ref, src_row, dst_ref, dst_row, sem):
    return pltpu.make_async_copy(src_ref.at[pl.ds(src_row, 1)], dst_ref.at[pl.ds(dst_row, 1)], sem)


def _dispatch_kernel(pos_ref, x_ref, g_ref, xs_in_ref, xs_ref, buf_ref, sem):
    del xs_in_ref
    i = pl.program_id(0)
    n = pl.num_programs(0)
    tm = x_ref.shape[0]
    slot = i % 2

    def wait_slot(sl):
        for _ in range(TOP_K):
            pltpu.make_async_copy(buf_ref.at[sl], xs_ref.at[pl.ds(0, tm)], sem.at[sl]).wait()

    @pl.when(i >= 2)
    def _():
        wait_slot(slot)

    buf_ref[slot] = _pack_rows(_rms(x_ref[...], g_ref[...]))

    def issue(r, c):
        for k in range(TOP_K):
            p = pos_ref[(i * tm + r) * TOP_K + k]
            _row_copy(buf_ref.at[slot], r, xs_ref, p, sem.at[slot]).start()
        return c

    lax.fori_loop(0, tm, issue, 0, unroll=ISSUE_UNROLL)

    @pl.when(i == n - 1)
    def _():
        wait_slot(slot)

    @pl.when((i == n - 1) & (n >= 2))
    def _():
        wait_slot(1 - slot)


def _dispatch(pos, x2, gain, n_slots):
    t, d = x2.shape
    tm = min(MOVE_TILE, t)
    zeros = jnp.zeros((n_slots, d // 2), U32)
    grid_spec = pltpu.PrefetchScalarGridSpec(
        num_scalar_prefetch=1,
        grid=(t // tm,),
        in_specs=[pl.BlockSpec((tm, d), lambda i, pos: (i, 0)),
                  pl.BlockSpec((1, d), lambda i, pos: (0, 0)),
                  pl.BlockSpec(memory_space=pl.ANY)],
        out_specs=pl.BlockSpec(memory_space=pl.ANY),
        scratch_shapes=[pltpu.VMEM((2, tm, d // 2), U32),
                        pltpu.SemaphoreType.DMA((2,))],
    )
    return pl.pallas_call(
        _dispatch_kernel,
        grid_spec=grid_spec,
        out_shape=jax.ShapeDtypeStruct((n_slots, d // 2), U32),
        input_output_aliases={3: 0},
        compiler_params=_params("arbitrary"),
        name="dispatch",
    )(pos, x2, gain, zeros)


def _expert_kernel(be_ref, nu_ref, xs_ref, wg_ref, wu_ref, wd_ref, y_ref, wgb_ref, wub_ref, wdb_ref):
    i = pl.program_id(0)
    e = be_ref[i]
    prev = be_ref[jnp.maximum(i - 1, 0)]

    @pl.when((i == 0) | (e != prev))
    def _():
        wgb_ref[...] = wg_ref[0, 0].astype(BF16)
        wub_ref[...] = wu_ref[0, 0].astype(BF16)
        wdb_ref[...] = wd_ref[0, 0].astype(BF16)

    @pl.when(i < nu_ref[0])
    def _():
        x = _unpack_rows(xs_ref[...]).astype(BF16)
        g = jnp.dot(x, wgb_ref[...], preferred_element_type=F32)
        u = jnp.dot(x, wub_ref[...], preferred_element_type=F32)
        a = (g / (1.0 + jnp.exp(-g)) * u).astype(BF16)
        y_ref[...] = _pack_rows(jnp.dot(a, wdb_ref[...], preferred_element_type=F32))

    @pl.when(i >= nu_ref[0])
    def _():
        y_ref[...] = jnp.zeros(y_ref.shape, U32)


def _experts(block_expert, n_used, xs, w_gate, w_up, w_down, layer):
    n_slots, dh = xs.shape
    blk = EXPERT_BLOCK
    d, ff = w_gate.shape[2], w_gate.shape[3]
    grid_spec = pltpu.PrefetchScalarGridSpec(
        num_scalar_prefetch=2,
        grid=(n_slots // blk,),
        in_specs=[pl.BlockSpec((blk, dh), lambda i, be, nu: (i, 0)),
                  pl.BlockSpec((1, 1, d, ff), lambda i, be, nu: (layer, be[i], 0, 0)),
                  pl.BlockSpec((1, 1, d, ff), lambda i, be, nu: (layer, be[i], 0, 0)),
                  pl.BlockSpec((1, 1, ff, d), lambda i, be, nu: (layer, be[i], 0, 0))],
        out_specs=pl.BlockSpec((blk, dh), lambda i, be, nu: (i, 0)),
        scratch_shapes=[pltpu.VMEM((d, ff), BF16),
                        pltpu.VMEM((d, ff), BF16),
                        pltpu.VMEM((ff, d), BF16)],
    )
    return pl.pallas_call(
        _expert_kernel,
        grid_spec=grid_spec,
        out_shape=jax.ShapeDtypeStruct((n_slots, dh), U32),
        compiler_params=_params("arbitrary"),
        name="experts",
    )(block_expert, n_used, xs, w_gate, w_up, w_down)


def _combine_kernel(pos_ref, y_ref, x_ref, w_ref, o_ref, buf_ref, sem):
    i = pl.program_id(0)
    n = pl.num_programs(0)
    tm = x_ref.shape[0]
    slot = i % 2

    def issue_tile(tile, sl):
        def body(r, c):
            for k in range(TOP_K):
                p = pos_ref[(tile * tm + r) * TOP_K + k]
                _row_copy(y_ref, p, buf_ref.at[sl], k * tm + r, sem.at[sl]).start()
            return c
        lax.fori_loop(0, tm, body, 0, unroll=ISSUE_UNROLL)

    @pl.when(i == 0)
    def _():
        issue_tile(0, 0)

    @pl.when(i + 1 < n)
    def _():
        issue_tile(i + 1, 1 - slot)

    pltpu.make_async_copy(y_ref.at[pl.ds(0, TOP_K * tm)], buf_ref.at[slot], sem.at[slot]).wait()

    rows = buf_ref[slot]
    w = w_ref[...]
    out = x_ref[...]
    for k in range(TOP_K):
        out = out + w[:, k:k + 1] * _unpack_rows(rows[k * tm:(k + 1) * tm])
    o_ref[...] = out


def _combine(pos, y, x2, weights):
    t, d = x2.shape
    tm = min(MOVE_TILE, t)
    grid_spec = pltpu.PrefetchScalarGridSpec(
        num_scalar_prefetch=1,
        grid=(t // tm,),
        in_specs=[pl.BlockSpec(memory_space=pl.ANY),
                  pl.BlockSpec((tm, d), lambda i, pos: (i, 0)),
                  pl.BlockSpec((tm, TOP_K), lambda i, pos: (i, 0))],
        out_specs=pl.BlockSpec((tm, d), lambda i, pos: (i, 0)),
        scratch_shapes=[pltpu.VMEM((2, TOP_K * tm, d // 2), U32),
                        pltpu.SemaphoreType.DMA((2,))],
    )
    return pl.pallas_call(
        _combine_kernel,
        grid_spec=grid_spec,
        out_shape=jax.ShapeDtypeStruct((t, d), F32),
        compiler_params=_params("arbitrary"),
        name="combine",
    )(pos, y, x2, weights)


def _hier_moe(x2, gain, wg1, bg1, wg2, bg2, w_gate, w_up, w_down, layer):
    t, d = x2.shape
    wr = jnp.zeros((d, ROUTER_COLS), F32)
    wr = wr.at[:, :N_EXPERTS].set(wg2.astype(F32)).at[:, N_EXPERTS:N_EXPERTS + N_EXPERT_GROUPS].set(wg1.astype(F32))
    br = jnp.zeros((1, ROUTER_COLS), F32)
    br = br.at[0, :N_EXPERTS].set(bg2.astype(F32)).at[0, N_EXPERTS:N_EXPERTS + N_EXPERT_GROUPS].set(bg1.astype(F32))
    route = _router(x2, gain, wr, br)

    blk = EXPERT_BLOCK
    n_blocks = -(-(t * TOP_K) // blk) + N_EXPERTS
    pos, block_expert, n_used = _slot_positions(route, blk, n_blocks)
    xs = _dispatch(pos, x2, gain, n_blocks * blk)
    y = _experts(block_expert, n_used, xs, w_gate, w_up, w_down, layer)
    return _combine(pos, y, x2, route[TOP_K:2 * TOP_K].T)


def kernel(x, rel_bias, attn_norm, w_qkv, q_gain, k_gain, lambda_q1, lambda_k1, lambda_q2, lambda_k2,
           subln_gain, w_o, pool_norm, pool_w_in, pool_w_group, pool_w_out, pool_scale,
           ffn_norm, router_group_w, router_group_b, router_expert_w, router_expert_b,
           w_gate, w_up, w_down):
    b, s, d = x.shape
    depth = ffn_norm.shape[0]
    x = x.astype(F32)
    bias_tiles = _bias_tiles(rel_bias)
    for i in range(depth):
        j = i // N_MIXERS
        if i % N_MIXERS == 0:
            lambda_init = 0.8 - 0.6 * math.exp(-0.3 * i)
            x2 = x.reshape(b * s, d)
            qkv = _norm_matmul(x2, attn_norm[j].reshape(1, d).astype(F32), w_qkv[j].astype(BF16))
            lam_params = jnp.stack([lambda_q1[j], lambda_k1[j], lambda_q2[j], lambda_k2[j]]).astype(F32)
            o = _attention(qkv.reshape(b, s, -1), bias_tiles,
                           jnp.tile(q_gain[j].astype(F32), 2).reshape(1, HEAD_W),
                           jnp.tile(k_gain[j].astype(F32), 2).reshape(1, HEAD_W),
                           lam_params, subln_gain[j].reshape(1, HEAD_W).astype(F32), lambda_init)
            x2 = _matmul_residual(o.reshape(b * s, -1), w_o[j].astype(BF16), x2)
        else:
            x3 = _pool_mixer(x, pool_norm[j].reshape(1, d).astype(F32), pool_w_in[j].astype(BF16),
                             pool_w_group[j].astype(BF16), pool_w_out[j].astype(BF16),
                             pool_scale[j].reshape(1, d).astype(F32))
            x2 = x3.reshape(b * s, d)
        x2 = _hier_moe(x2, ffn_norm[i].reshape(1, d).astype(F32), router_group_w[i], router_group_b[i],
                       router_expert_w[i], router_expert_b[i], w_gate, w_up, w_down, i)
        x = x2.reshape(b, s, d)
    return x
```

```python
import functools
import math

import jax
import jax.numpy as jnp
from jax import lax
from jax.experimental import pallas as pl
from jax.experimental.pallas import tpu as pltpu

F32 = jnp.float32
BF16 = jnp.bfloat16
U32 = jnp.uint32
I32 = jnp.int32

EPS = 1e-6
HEAD_DIM = 64
HEAD_W = 2 * HEAD_DIM
NUM_BUCKETS = 32
MAX_EXACT = NUM_BUCKETS // 2
MAX_DISTANCE = 128
POOL_WINDOWS = (2, 4, 8, 16)
N_EXPERT_GROUPS = 4
EXPERTS_PER_GROUP = 8
N_EXPERTS = N_EXPERT_GROUPS * EXPERTS_PER_GROUP
TOP_K = 2
N_MIXERS = 2
LOG2E = math.log2(math.e)

LANES = 128
VMEM_LIMIT = 48 * 1024 * 1024
ROW_TILE = 512
ATTN_TILE = 512
ATTN_PREP_ROWS = 256
VT_ROWS = HEAD_W + 16
EXPERT_BLOCK = 512
MOVE_TILE = 256
ISSUE_UNROLL = 8
POOL_HALO = 16
MASK_VALUE = -1e30
ROUTER_COLS = LANES


def _params(*sem):
    return pltpu.CompilerParams(dimension_semantics=sem, vmem_limit_bytes=VMEM_LIMIT)


def _rms(x, gain):
    ms = jnp.mean(x * x, axis=-1, keepdims=True)
    return x * lax.rsqrt(ms + EPS) * gain


def _pack_rows(v):
    c = v.shape[1] // 2
    bits = lax.bitcast_convert_type(v.astype(BF16).astype(F32), U32)
    return (bits[:, :c] >> 16) | (bits[:, c:] & jnp.uint32(0xFFFF0000))


def _unpack_rows(w):
    lo = lax.bitcast_convert_type(w << 16, F32)
    hi = lax.bitcast_convert_type(w & jnp.uint32(0xFFFF0000), F32)
    return jnp.concatenate([lo, hi], axis=1)


def _norm_matmul_kernel(x_ref, g_ref, w_ref, o_ref, *, n_chunks):
    h = _rms(x_ref[...], g_ref[...]).astype(BF16)
    cw = o_ref.shape[1] // n_chunks
    for c in range(n_chunks):
        o_ref[:, c * cw:(c + 1) * cw] = jnp.dot(
            h, w_ref[:, c * cw:(c + 1) * cw], preferred_element_type=F32).astype(o_ref.dtype)


def _norm_matmul(x2, gain, w):
    t, d = x2.shape
    n = w.shape[1]
    tm = min(ROW_TILE, t)
    return pl.pallas_call(
        functools.partial(_norm_matmul_kernel, n_chunks=n // d),
        grid=(t // tm,),
        in_specs=[pl.BlockSpec((tm, d), lambda i: (i, 0)),
                  pl.BlockSpec((1, d), lambda i: (0, 0)),
                  pl.BlockSpec((d, n), lambda i: (0, 0))],
        out_specs=pl.BlockSpec((tm, n), lambda i: (i, 0)),
        out_shape=jax.ShapeDtypeStruct((t, n), BF16),
        compiler_params=_params("parallel"),
        name="norm_matmul",
    )(x2, gain, w)


def _matmul_residual_kernel(a_ref, w_ref, x_ref, o_ref):
    o_ref[...] = x_ref[...] + jnp.dot(a_ref[...], w_ref[...], preferred_element_type=F32)


def _matmul_residual(a, w, x2):
    t, d = x2.shape
    k = a.shape[1]
    tm = min(ROW_TILE, t)
    return pl.pallas_call(
        _matmul_residual_kernel,
        grid=(t // tm,),
        in_specs=[pl.BlockSpec((tm, k), lambda i: (i, 0)),
                  pl.BlockSpec((k, d), lambda i: (0, 0)),
                  pl.BlockSpec((tm, d), lambda i: (i, 0))],
        out_specs=pl.BlockSpec((tm, d), lambda i: (i, 0)),
        out_shape=jax.ShapeDtypeStruct((t, d), F32),
        compiler_params=_params("parallel"),
        name="matmul_residual",
    )(a, w, x2)


def _rel_bucket(dist):
    n = jnp.maximum(dist, 0)
    nf = jnp.maximum(n, 1).astype(F32)
    large = MAX_EXACT + (jnp.log(nf / MAX_EXACT) / math.log(MAX_DISTANCE / MAX_EXACT)
                         * (NUM_BUCKETS - MAX_EXACT)).astype(I32)
    large = jnp.minimum(large, NUM_BUCKETS - 1)
    return jnp.where(n < MAX_EXACT, n, large)


def _bias_tiles(rel_bias):
    assert ATTN_TILE >= MAX_DISTANCE
    kj = jnp.arange(ATTN_TILE)[:, None]
    qi = jnp.arange(ATTN_TILE)[None, :]
    table = rel_bias.astype(F32)
    vals = (table - table[NUM_BUCKETS - 1]) * LOG2E
    tiles = []
    for off in (0, ATTN_TILE):
        dist = qi - kj + off
        bucket = _rel_bucket(dist)[None]
        b = jnp.zeros((table.shape[1], ATTN_TILE, ATTN_TILE), F32)
        for n in range(NUM_BUCKETS):
            b = jnp.where(bucket == n, vals[n][:, None, None], b)
        tiles.append(jnp.where((dist >= 0)[None], b, MASK_VALUE))
    return jnp.stack(tiles, axis=1)


def _attn_kernel(q_ref, k_ref, v_ref, bias_ref, qg_ref, kg_ref, lam_ref, sg_ref, o_ref,
                 kn_ref, vt_ref, qpt1_ref, qpt2_ref, sa1_ref, sa2_ref, sb1_ref, sb2_ref,
                 acc1_ref, acc2_ref, m1_ref, m2_ref, *, lambda_init):
    qpt_refs, acc_refs, m_refs = (qpt1_ref, qpt2_ref), (acc1_ref, acc2_ref), (m1_ref, m2_ref)
    buf_a, buf_b = (sa1_ref, sa2_ref), (sb1_ref, sb2_ref)
    seq = k_ref.shape[1]
    tq = ATTN_TILE
    pr = ATTN_PREP_ROWS
    first = lax.broadcasted_iota(I32, (1, HEAD_W), 1) < HEAD_DIM

    def half_norm(x, gain):
        sq = x * x
        s1 = jnp.sum(jnp.where(first, sq, 0.0), axis=-1, keepdims=True)
        s2 = jnp.sum(jnp.where(first, 0.0, sq), axis=-1, keepdims=True)
        r = jnp.where(first, lax.rsqrt(s1 / HEAD_DIM + EPS), lax.rsqrt(s2 / HEAD_DIM + EPS))
        return x * r * gain

    def prep(c, carry):
        r0 = pl.multiple_of(c * pr, pr)
        k = k_ref[0, pl.ds(r0, pr), :].astype(F32)
        kn_ref[pl.ds(r0, pr), :] = half_norm(k, kg_ref[...]).astype(BF16)
        vt_ref[0:HEAD_W, pl.ds(r0, pr)] = v_ref[0, pl.ds(r0, pr), :].astype(F32).T.astype(BF16)
        return carry

    lax.fori_loop(0, seq // pr, prep, 0)
    ones_row = lax.broadcasted_iota(I32, (VT_ROWS - HEAD_W, seq), 0) == 0
    vt_ref[HEAD_W:VT_ROWS, :] = jnp.where(ones_row, 1.0, 0.0).astype(BF16)

    lp = lam_ref[...]
    lam = (jnp.exp(jnp.sum(lp[0:1] * lp[1:2], axis=-1, keepdims=True))
           - jnp.exp(jnp.sum(lp[2:3] * lp[3:4], axis=-1, keepdims=True)) + lambda_init)
    first_rows = lax.broadcasted_iota(I32, (HEAD_W, 1), 0) < HEAD_DIM

    def logits(kj, dst):
        k0 = pl.multiple_of(kj * tq, tq)
        kt = kn_ref[pl.ds(k0, tq), :]
        for c in range(2):
            dst[c][...] = jnp.dot(kt, qpt_refs[c][...], preferred_element_type=F32)

    def absorb(kj, src, bias):
        k0 = pl.multiple_of(kj * tq, tq)
        vt = vt_ref[:, pl.ds(k0, tq)]
        for c in range(2):
            s = src[c][...]
            if bias is not None:
                s = s + bias
            m_old = m_refs[c][...]
            m_new = jnp.maximum(m_old, jnp.max(s, axis=0, keepdims=True))
            p = jnp.exp2(s - m_new).astype(BF16)
            pv = jnp.dot(vt, p, preferred_element_type=F32)
            acc_refs[c][...] = jnp.exp2(m_old - m_new) * acc_refs[c][...] + pv
            m_refs[c][...] = m_new

    def query_tile(qi, carry):
        q0 = pl.multiple_of(qi * tq, tq)
        qn = half_norm(q_ref[0, pl.ds(q0, tq), :].astype(F32), qg_ref[...]) * (HEAD_DIM ** -0.5 * LOG2E)
        qnt = qn.T
        qpt1_ref[...] = jnp.where(first_rows, qnt, 0.0).astype(BF16)
        qpt2_ref[...] = jnp.where(first_rows, 0.0, qnt).astype(BF16)
        for c in range(2):
            m_refs[c][...] = jnp.full(m_refs[c].shape, MASK_VALUE, F32)
            acc_refs[c][...] = jnp.zeros(acc_refs[c].shape, F32)

        n_far = jnp.maximum(qi - 1, 0)
        logits(0, buf_a)

        def far_pair(t, c):
            logits(2 * t + 1, buf_b)
            absorb(2 * t, buf_a, None)
            logits(2 * t + 2, buf_a)
            absorb(2 * t + 1, buf_b, None)
            return c

        lax.fori_loop(0, n_far // 2, far_pair, 0)
        odd = n_far % 2 == 1

        @pl.when(odd)
        def _():
            logits(n_far, buf_b)
            absorb(n_far - 1, buf_a, None)

        @pl.when(qi == 0)
        def _():
            absorb(0, buf_a, bias_ref[0, 0])

        def near(cur, nxt):
            logits(qi, nxt)
            absorb(qi - 1, cur, bias_ref[0, 1])
            absorb(qi, nxt, bias_ref[0, 0])

        @pl.when((qi >= 1) & jnp.logical_not(odd))
        def _():
            near(buf_a, buf_b)

        @pl.when((qi >= 1) & odd)
        def _():
            near(buf_b, buf_a)

        a1 = acc1_ref[...]
        a2 = acc2_ref[...]
        o = (a1[:HEAD_W] / a1[HEAD_W:HEAD_W + 1] - lam * (a2[:HEAD_W] / a2[HEAD_W:HEAD_W + 1])).T
        o = _rms(o, sg_ref[...]) * (1.0 - lambda_init)
        o_ref[0, pl.ds(q0, tq), :] = o.astype(o_ref.dtype)
        return carry

    lax.fori_loop(0, seq // tq, query_tile, 0)


def _attention(qkv, bias_tiles, q_gain2, k_gain2, lam_params, subln_gain, lambda_init):
    b, s, w3 = qkv.shape
    w = w3 // 3
    n_heads = w // HEAD_W
    assert s % ATTN_TILE == 0
    tq = ATTN_TILE
    return pl.pallas_call(
        functools.partial(_attn_kernel, lambda_init=lambda_init),
        grid=(b, n_heads),
        in_specs=[pl.BlockSpec((1, s, HEAD_W), lambda bi, h: (bi, 0, h)),
                  pl.BlockSpec((1, s, HEAD_W), lambda bi, h: (bi, 0, n_heads + h)),
                  pl.BlockSpec((1, s, HEAD_W), lambda bi, h: (bi, 0, 2 * n_heads + h)),
                  pl.BlockSpec((1, 2, tq, tq), lambda bi, h: (h, 0, 0, 0)),
                  pl.BlockSpec((1, HEAD_W), lambda bi, h: (0, 0)),
                  pl.BlockSpec((1, HEAD_W), lambda bi, h: (0, 0)),
                  pl.BlockSpec((4, HEAD_DIM), lambda bi, h: (0, 0)),
                  pl.BlockSpec((1, HEAD_W), lambda bi, h: (0, 0))],
        out_specs=pl.BlockSpec((1, s, HEAD_W), lambda bi, h: (bi, 0, h)),
        out_shape=jax.ShapeDtypeStruct((b, s, w), BF16),
        scratch_shapes=[pltpu.VMEM((s, HEAD_W), BF16),
                        pltpu.VMEM((VT_ROWS, s), BF16)]
        + [pltpu.VMEM((HEAD_W, tq), BF16)] * 2
        + [pltpu.VMEM((tq, tq), F32)] * 4
        + [pltpu.VMEM((VT_ROWS, tq), F32)] * 2
        + [pltpu.VMEM((1, tq), F32)] * 2,
        compiler_params=_params("parallel", "parallel"),
        name="diff_attention",
    )(qkv, qkv, qkv, bias_tiles, q_gain2, k_gain2, lam_params, subln_gain)


def _pool_kernel(x_ref, g_ref, win_ref, wgrp_ref, wout_ref, scale_ref, o_ref, ext_ref, pooled_ref):
    j = pl.program_id(1)
    tm = x_ref.shape[1]
    d = x_ref.shape[2]
    gw = d // len(POOL_WINDOWS)
    x = x_ref[0]

    @pl.when(j == 0)
    def _():
        ext_ref[0:POOL_HALO, :] = jnp.zeros((POOL_HALO, d), F32)

    @pl.when(j > 0)
    def _():
        ext_ref[0:POOL_HALO, :] = ext_ref[tm:tm + POOL_HALO, :]

    h = _rms(x, g_ref[...]).astype(BF16)
    ext_ref[POOL_HALO:POOL_HALO + tm, :] = jnp.dot(h, win_ref[...], preferred_element_type=F32)

    pos1 = (j * tm + 1 + lax.broadcasted_iota(I32, (tm, 1), 0)).astype(F32)
    for g, win in enumerate(POOL_WINDOWS):
        c0, c1 = g * gw, (g + 1) * gw
        u = ext_ref[POOL_HALO:POOL_HALO + tm, c0:c1]
        s = u
        for k in range(1, win):
            s = s + ext_ref[POOL_HALO - k:POOL_HALO - k + tm, c0:c1]
        inv_cnt = 1.0 / jnp.minimum(pos1, float(win))
        pooled = (s * inv_cnt - u).astype(BF16)
        pooled_ref[:, c0:c1] = jnp.dot(pooled, wgrp_ref[g], preferred_element_type=F32).astype(BF16)

    y = jnp.dot(pooled_ref[...], wout_ref[...], preferred_element_type=F32)
    o_ref[0] = x + y * scale_ref[...]


def _pool_mixer(x, gain, w_in, w_group, w_out, scale):
    b, s, d = x.shape
    tm = min(ROW_TILE, s)
    ng = len(POOL_WINDOWS)
    gw = d // ng
    return pl.pallas_call(
        _pool_kernel,
        grid=(b, s // tm),
        in_specs=[pl.BlockSpec((1, tm, d), lambda bi, j: (bi, j, 0)),
                  pl.BlockSpec((1, d), lambda bi, j: (0, 0)),
                  pl.BlockSpec((d, d), lambda bi, j: (0, 0)),
                  pl.BlockSpec((ng, gw, gw), lambda bi, j: (0, 0, 0)),
                  pl.BlockSpec((d, d), lambda bi, j: (0, 0)),
                  pl.BlockSpec((1, d), lambda bi, j: (0, 0))],
        out_specs=pl.BlockSpec((1, tm, d), lambda bi, j: (bi, j, 0)),
        out_shape=jax.ShapeDtypeStruct((b, s, d), F32),
        scratch_shapes=[pltpu.VMEM((POOL_HALO + tm, d), F32),
                        pltpu.VMEM((tm, d), BF16)],
        compiler_params=_params("arbitrary", "arbitrary"),
        name="pool_mixer",
    )(x, gain, w_in, w_group, w_out, scale)


def _router_kernel(x_ref, g_ref, wr_ref, br_ref, o_ref, cnt_ref, carry_ref):
    tm = x_ref.shape[0]

    @pl.when(pl.program_id(0) == 0)
    def _():
        carry_ref[...] = jnp.zeros(carry_ref.shape, F32)

    h = _rms(x_ref[...], g_ref[...])
    logits = jnp.dot(h, wr_ref[...], preferred_element_type=F32,
                     precision=lax.Precision.HIGHEST) + br_ref[...]
    lt = logits.T

    def row(r):
        return lt[r:r + 1, :]

    g_best = row(N_EXPERTS)
    g_idx = jnp.zeros((1, tm), I32)
    for g in range(1, N_EXPERT_GROUPS):
        v = row(N_EXPERTS + g)
        better = v > g_best
        g_best = jnp.where(better, v, g_best)
        g_idx = jnp.where(better, g, g_idx)
    denom = jnp.zeros((1, tm), F32)
    for g in range(N_EXPERT_GROUPS):
        denom = denom + jnp.exp(row(N_EXPERTS + g) - g_best)
    g_prob = 1.0 / denom

    sel = []
    for e in range(EXPERTS_PER_GROUP):
        v = row(e)
        for g in range(1, N_EXPERT_GROUPS):
            v = jnp.where(g_idx == g, row(g * EXPERTS_PER_GROUP + e), v)
        sel.append(v)

    def top1(vals):
        best, idx = vals[0], jnp.zeros((1, tm), I32)
        for e in range(1, EXPERTS_PER_GROUP):
            better = vals[e] > best
            best = jnp.where(better, vals[e], best)
            idx = jnp.where(better, e, idx)
        return best, idx

    v0, i0 = top1(sel)
    v1, i1 = top1([jnp.where(i0 == e, -jnp.inf, sel[e]) for e in range(EXPERTS_PER_GROUP)])
    t = jnp.exp(v1 - v0)
    w0 = g_prob / (1.0 + t)
    w1 = w0 * t
    e0 = g_idx * EXPERTS_PER_GROUP + i0
    e1 = g_idx * EXPERTS_PER_GROUP + i1

    eid = lax.broadcasted_iota(I32, (N_EXPERTS, tm), 0)
    oh0 = eid == e0
    oh1 = eid == e1
    onehot = jnp.where(oh0 | oh1, 1.0, 0.0)
    earlier = (lax.broadcasted_iota(I32, (tm, tm), 0) < lax.broadcasted_iota(I32, (tm, tm), 1))
    prefix = jnp.dot(onehot.astype(BF16), jnp.where(earlier, 1.0, 0.0).astype(BF16),
                     preferred_element_type=F32)
    before = prefix + carry_ref[:, 0:1]
    rank0 = jnp.sum(jnp.where(oh0, before, 0.0), axis=0, keepdims=True)
    rank1 = jnp.sum(jnp.where(oh1, before, 0.0), axis=0, keepdims=True)
    carry_ref[...] = carry_ref[...] + jnp.sum(onehot, axis=1, keepdims=True)
    cnt_ref[...] = carry_ref[...]

    o_ref[...] = jnp.concatenate(
        [e0.astype(F32), e1.astype(F32), w0, w1, rank0, rank1, jnp.zeros((2, tm), F32)], axis=0)


def _router(x2, gain, wr, br):
    t, d = x2.shape
    tm = min(ROW_TILE, t)
    assert t * TOP_K < 2 ** 24
    return pl.pallas_call(
        _router_kernel,
        grid=(t // tm,),
        in_specs=[pl.BlockSpec((tm, d), lambda i: (i, 0)),
                  pl.BlockSpec((1, d), lambda i: (0, 0)),
                  pl.BlockSpec((d, ROUTER_COLS), lambda i: (0, 0)),
                  pl.BlockSpec((1, ROUTER_COLS), lambda i: (0, 0))],
        out_specs=[pl.BlockSpec((8, tm), lambda i: (0, i)),
                   pl.BlockSpec((N_EXPERTS, LANES), lambda i: (0, 0))],
        out_shape=[jax.ShapeDtypeStruct((8, t), F32),
                   jax.ShapeDtypeStruct((N_EXPERTS, LANES), F32)],
        scratch_shapes=[pltpu.VMEM((N_EXPERTS, LANES), F32)],
        compiler_params=_params("arbitrary"),
        name="router",
    )(x2, gain, wr, br)


def _slot_positions(route, counts_f, blk, n_blocks):
    e = route[0:TOP_K].astype(I32).T.reshape(-1)
    rank = route[2 * TOP_K:3 * TOP_K].astype(I32).T.reshape(-1)
    counts = counts_f[:, 0].astype(I32)
    padded = (counts + blk - 1) // blk * blk
    pad_end = jnp.cumsum(padded)
    pos = (pad_end - padded)[e] + rank
    block_expert = jnp.minimum(
        jnp.searchsorted(pad_end, jnp.arange(n_blocks, dtype=I32) * blk, side="right"), N_EXPERTS - 1)
    n_used = (pad_end[-1] // blk).reshape(1)
    return pos.astype(I32), block_expert.astype(I32), n_used.astype(I32)


def _row_copy(src_ref, src_row, dst_ref, dst_row, sem):
    return pltpu.make_async_copy(src_ref.at[pl.ds(src_row, 1)], dst_ref.at[pl.ds(dst_row, 1)], sem)


def _dispatch_kernel(pos_ref, x_ref, g_ref, xs_in_ref, xs_ref, buf_ref, sem):
    del xs_in_ref
    i = pl.program_id(0)
    n = pl.num_programs(0)
    tm = x_ref.shape[0]
    slot = i % 2

    def wait_slot(sl):
        for _ in range(TOP_K):
            pltpu.make_async_copy(buf_ref.at[sl], xs_ref.at[pl.ds(0, tm)], sem.at[sl]).wait()

    @pl.when(i >= 2)
    def _():
        wait_slot(slot)

    buf_ref[slot] = _pack_rows(_rms(x_ref[...], g_ref[...]))

    def issue(r, c):
        for k in range(TOP_K):
            p = pos_ref[(i * tm + r) * TOP_K + k]
            _row_copy(buf_ref.at[slot], r, xs_ref, p, sem.at[slot]).start()
        return c

    lax.fori_loop(0, tm, issue, 0, unroll=ISSUE_UNROLL)

    @pl.when(i == n - 1)
    def _():
        wait_slot(slot)

    @pl.when((i == n - 1) & (n >= 2))
    def _():
        wait_slot(1 - slot)


def _dispatch(pos, x2, gain, n_slots):
    t, d = x2.shape
    tm = min(MOVE_TILE, t)
    zeros = jnp.zeros((n_slots, d // 2), U32)
    grid_spec = pltpu.PrefetchScalarGridSpec(
        num_scalar_prefetch=1,
        grid=(t // tm,),
        in_specs=[pl.BlockSpec((tm, d), lambda i, pos: (i, 0)),
                  pl.BlockSpec((1, d), lambda i, pos: (0, 0)),
                  pl.BlockSpec(memory_space=pl.ANY)],
        out_specs=pl.BlockSpec(memory_space=pl.ANY),
        scratch_shapes=[pltpu.VMEM((2, tm, d // 2), U32),
                        pltpu.SemaphoreType.DMA((2,))],
    )
    return pl.pallas_call(
        _dispatch_kernel,
        grid_spec=grid_spec,
        out_shape=jax.ShapeDtypeStruct((n_slots, d // 2), U32),
        input_output_aliases={3: 0},
        compiler_params=_params("arbitrary"),
        name="dispatch",
    )(pos, x2, gain, zeros)


def _expert_kernel(be_ref, nu_ref, xs_ref, wg_ref, wu_ref, wd_ref, y_ref, wgb_ref, wub_ref, wdb_ref):
    i = pl.program_id(0)
    e = be_ref[i]
    prev = be_ref[jnp.maximum(i - 1, 0)]

    @pl.when((i == 0) | (e != prev))
    def _():
        wgb_ref[...] = wg_ref[0, 0].astype(BF16)
        wub_ref[...] = wu_ref[0, 0].astype(BF16)
        wdb_ref[...] = wd_ref[0, 0].astype(BF16)

    @pl.when(i < nu_ref[0])
    def _():
        x = _unpack_rows(xs_ref[...]).astype(BF16)
        g = jnp.dot(x, wgb_ref[...], preferred_element_type=F32)
        u = jnp.dot(x, wub_ref[...], preferred_element_type=F32)
        a = (g / (1.0 + jnp.exp(-g)) * u).astype(BF16)
        y_ref[...] = _pack_rows(jnp.dot(a, wdb_ref[...], preferred_element_type=F32))

    @pl.when(i >= nu_ref[0])
    def _():
        y_ref[...] = jnp.zeros(y_ref.shape, U32)


def _experts(block_expert, n_used, xs, w_gate, w_up, w_down, layer):
    n_slots, dh = xs.shape
    blk = EXPERT_BLOCK
    d, ff = w_gate.shape[2], w_gate.shape[3]
    grid_spec = pltpu.PrefetchScalarGridSpec(
        num_scalar_prefetch=2,
        grid=(n_slots // blk,),
        in_specs=[pl.BlockSpec((blk, dh), lambda i, be, nu: (i, 0)),
                  pl.BlockSpec((1, 1, d, ff), lambda i, be, nu: (layer, be[i], 0, 0)),
                  pl.BlockSpec((1, 1, d, ff), lambda i, be, nu: (layer, be[i], 0, 0)),
                  pl.BlockSpec((1, 1, ff, d), lambda i, be, nu: (layer, be[i], 0, 0))],
        out_specs=pl.BlockSpec((blk, dh), lambda i, be, nu: (i, 0)),
        scratch_shapes=[pltpu.VMEM((d, ff), BF16),
                        pltpu.VMEM((d, ff), BF16),
                        pltpu.VMEM((ff, d), BF16)],
    )
    return pl.pallas_call(
        _expert_kernel,
        grid_spec=grid_spec,
        out_shape=jax.ShapeDtypeStruct((n_slots, dh), U32),
        compiler_params=_params("arbitrary"),
        name="experts",
    )(block_expert, n_used, xs, w_gate, w_up, w_down)


def _combine_kernel(pos_ref, y_ref, x_ref, w_ref, o_ref, buf_ref, sem):
    i = pl.program_id(0)
    n = pl.num_programs(0)
    tm = x_ref.shape[0]
    slot = i % 2

    def issue_tile(tile, sl):
        def body(r, c):
            for k in range(TOP_K):
                p = pos_ref[(tile * tm + r) * TOP_K + k]
                _row_copy(y_ref, p, buf_ref.at[sl], k * tm + r, sem.at[sl]).start()
            return c
        lax.fori_loop(0, tm, body, 0, unroll=ISSUE_UNROLL)

    @pl.when(i == 0)
    def _():
        issue_tile(0, 0)

    @pl.when(i + 1 < n)
    def _():
        issue_tile(i + 1, 1 - slot)

    pltpu.make_async_copy(y_ref.at[pl.ds(0, TOP_K * tm)], buf_ref.at[slot], sem.at[slot]).wait()

    rows = buf_ref[slot]
    w = w_ref[...]
    out = x_ref[...]
    for k in range(TOP_K):
        out = out + w[:, k:k + 1] * _unpack_rows(rows[k * tm:(k + 1) * tm])
    o_ref[...] = out


def _combine(pos, y, x2, weights):
    t, d = x2.shape
    tm = min(MOVE_TILE, t)
    grid_spec = pltpu.PrefetchScalarGridSpec(
        num_scalar_prefetch=1,
        grid=(t // tm,),
        in_specs=[pl.BlockSpec(memory_space=pl.ANY),
                  pl.BlockSpec((tm, d), lambda i, pos: (i, 0)),
                  pl.BlockSpec((tm, TOP_K), lambda i, pos: (i, 0))],
        out_specs=pl.BlockSpec((tm, d), lambda i, pos: (i, 0)),
        scratch_shapes=[pltpu.VMEM((2, TOP_K * tm, d // 2), U32),
                        pltpu.SemaphoreType.DMA((2,))],
    )
    return pl.pallas_call(
        _combine_kernel,
        grid_spec=grid_spec,
        out_shape=jax.ShapeDtypeStruct((t, d), F32),
        compiler_params=_params("arbitrary"),
        name="combine",
    )(pos, y, x2, weights)


def _hier_moe(x2, gain, wg1, bg1, wg2, bg2, w_gate, w_up, w_down, layer):
    t, d = x2.shape
    wr = jnp.zeros((d, ROUTER_COLS), F32)
    wr = wr.at[:, :N_EXPERTS].set(wg2.astype(F32)).at[:, N_EXPERTS:N_EXPERTS + N_EXPERT_GROUPS].set(wg1.astype(F32))
    br = jnp.zeros((1, ROUTER_COLS), F32)
    br = br.at[0, :N_EXPERTS].set(bg2.astype(F32)).at[0, N_EXPERTS:N_EXPERTS + N_EXPERT_GROUPS].set(bg1.astype(F32))
    route, counts = _router(x2, gain, wr, br)

    blk = EXPERT_BLOCK
    n_blocks = -(-(t * TOP_K) // blk) + N_EXPERTS
    pos, block_expert, n_used = _slot_positions(route, counts, blk, n_blocks)
    xs = _dispatch(pos, x2, gain, n_blocks * blk)
    y = _experts(block_expert, n_used, xs, w_gate, w_up, w_down, layer)
    return _combine(pos, y, x2, route[TOP_K:2 * TOP_K].T)


def kernel(x, rel_bias, attn_norm, w_qkv, q_gain, k_gain, lambda_q1, lambda_k1, lambda_q2, lambda_k2,
           subln_gain, w_o, pool_norm, pool_w_in, pool_w_group, pool_w_out, pool_scale,
           ffn_norm, router_group_w, router_group_b, router_expert_w, router_expert_b,
           w_gate, w_up, w_down):
    b, s, d = x.shape
    depth = ffn_norm.shape[0]
    x = x.astype(F32)
    bias_tiles = _bias_tiles(rel_bias)
    for i in range(depth):
        j = i // N_MIXERS
        if i % N_MIXERS == 0:
            lambda_init = 0.8 - 0.6 * math.exp(-0.3 * i)
            x2 = x.reshape(b * s, d)
            qkv = _norm_matmul(x2, attn_norm[j].reshape(1, d).astype(F32), w_qkv[j].astype(BF16))
            lam_params = jnp.stack([lambda_q1[j], lambda_k1[j], lambda_q2[j], lambda_k2[j]]).astype(F32)
            o = _attention(qkv.reshape(b, s, -1), bias_tiles,
                           jnp.tile(q_gain[j].astype(F32), 2).reshape(1, HEAD_W),
                           jnp.tile(k_gain[j].astype(F32), 2).reshape(1, HEAD_W),
                           lam_params, subln_gain[j].reshape(1, HEAD_W).astype(F32), lambda_init)
            x2 = _matmul_residual(o.reshape(b * s, -1), w_o[j].astype(BF16), x2)
        else:
            x3 = _pool_mixer(x, pool_norm[j].reshape(1, d).astype(F32), pool_w_in[j].astype(BF16),
                             pool_w_group[j].astype(BF16), pool_w_out[j].astype(BF16),
                             pool_scale[j].reshape(1, d).astype(F32))
            x2 = x3.reshape(b * s, d)
        x2 = _hier_moe(x2, ffn_norm[i].reshape(1, d).astype(F32), router_group_w[i], router_group_b[i],
                       router_expert_w[i], router_expert_b[i], w_gate, w_up, w_down, i)
        x = x2.reshape(b, s, d)
    return x
```

```python
import functools
import math

import jax
import jax.numpy as jnp
from jax import lax
from jax.experimental import pallas as pl
from jax.experimental.pallas import tpu as pltpu

F32 = jnp.float32
BF16 = jnp.bfloat16
U32 = jnp.uint32
I32 = jnp.int32

EPS = 1e-6
HEAD_DIM = 64
HEAD_W = 2 * HEAD_DIM
NUM_BUCKETS = 32
MAX_EXACT = NUM_BUCKETS // 2
MAX_DISTANCE = 128
POOL_WINDOWS = (2, 4, 8, 16)
N_EXPERT_GROUPS = 4
EXPERTS_PER_GROUP = 8
N_EXPERTS = N_EXPERT_GROUPS * EXPERTS_PER_GROUP
TOP_K = 2
N_MIXERS = 2
LOG2E = math.log2(math.e)

LANES = 128
VMEM_LIMIT = 48 * 1024 * 1024
ROW_TILE = 512
ATTN_TILE = 512
ATTN_PREP_ROWS = 256
VT_ROWS = HEAD_W + 16
EXPERT_BLOCK = 512
MOVE_TILE = 256
ISSUE_UNROLL = 8
POOL_HALO = 16
MASK_VALUE = -1e30
FAR, SUBDIAG, DIAG = 0, 1, 2
ROUTER_COLS = LANES


def _params(*sem):
    return pltpu.CompilerParams(dimension_semantics=sem, vmem_limit_bytes=VMEM_LIMIT)


def _rms(x, gain):
    ms = jnp.mean(x * x, axis=-1, keepdims=True)
    return x * lax.rsqrt(ms + EPS) * gain


def _pack_rows(v):
    c = v.shape[1] // 2
    bits = lax.bitcast_convert_type(v.astype(BF16).astype(F32), U32)
    return (bits[:, :c] >> 16) | (bits[:, c:] & jnp.uint32(0xFFFF0000))


def _unpack_rows(w):
    lo = lax.bitcast_convert_type(w << 16, F32)
    hi = lax.bitcast_convert_type(w & jnp.uint32(0xFFFF0000), F32)
    return jnp.concatenate([lo, hi], axis=1)


def _norm_matmul_kernel(x_ref, g_ref, w_ref, o_ref, *, n_chunks):
    h = _rms(x_ref[...], g_ref[...]).astype(BF16)
    cw = o_ref.shape[1] // n_chunks
    for c in range(n_chunks):
        o_ref[:, c * cw:(c + 1) * cw] = jnp.dot(
            h, w_ref[:, c * cw:(c + 1) * cw], preferred_element_type=F32).astype(o_ref.dtype)


def _norm_matmul(x2, gain, w):
    t, d = x2.shape
    n = w.shape[1]
    tm = min(ROW_TILE, t)
    return pl.pallas_call(
        functools.partial(_norm_matmul_kernel, n_chunks=n // d),
        grid=(t // tm,),
        in_specs=[pl.BlockSpec((tm, d), lambda i: (i, 0)),
                  pl.BlockSpec((1, d), lambda i: (0, 0)),
                  pl.BlockSpec((d, n), lambda i: (0, 0))],
        out_specs=pl.BlockSpec((tm, n), lambda i: (i, 0)),
        out_shape=jax.ShapeDtypeStruct((t, n), BF16),
        compiler_params=_params("parallel"),
        name="norm_matmul",
    )(x2, gain, w)


def _matmul_residual_kernel(a_ref, w_ref, x_ref, o_ref):
    o_ref[...] = x_ref[...] + jnp.dot(a_ref[...], w_ref[...], preferred_element_type=F32)


def _matmul_residual(a, w, x2):
    t, d = x2.shape
    k = a.shape[1]
    tm = min(ROW_TILE, t)
    return pl.pallas_call(
        _matmul_residual_kernel,
        grid=(t // tm,),
        in_specs=[pl.BlockSpec((tm, k), lambda i: (i, 0)),
                  pl.BlockSpec((k, d), lambda i: (0, 0)),
                  pl.BlockSpec((tm, d), lambda i: (i, 0))],
        out_specs=pl.BlockSpec((tm, d), lambda i: (i, 0)),
        out_shape=jax.ShapeDtypeStruct((t, d), F32),
        compiler_params=_params("parallel"),
        name="matmul_residual",
    )(a, w, x2)


def _rel_bucket(dist):
    n = jnp.maximum(dist, 0)
    nf = jnp.maximum(n, 1).astype(F32)
    large = MAX_EXACT + (jnp.log(nf / MAX_EXACT) / math.log(MAX_DISTANCE / MAX_EXACT)
                         * (NUM_BUCKETS - MAX_EXACT)).astype(I32)
    large = jnp.minimum(large, NUM_BUCKETS - 1)
    return jnp.where(n < MAX_EXACT, n, large)


def _bias_tiles(rel_bias):
    assert ATTN_TILE >= MAX_DISTANCE
    kj = jnp.arange(ATTN_TILE)[:, None]
    qi = jnp.arange(ATTN_TILE)[None, :]
    table = rel_bias.astype(F32)
    vals = (table - table[NUM_BUCKETS - 1]) * LOG2E
    tiles = []
    for off in (0, ATTN_TILE):
        dist = qi - kj + off
        bucket = _rel_bucket(dist)[None]
        b = jnp.zeros((table.shape[1], ATTN_TILE, ATTN_TILE), F32)
        for n in range(NUM_BUCKETS):
            b = jnp.where(bucket == n, vals[n][:, None, None], b)
        tiles.append(jnp.where((dist >= 0)[None], b, MASK_VALUE))
    return jnp.stack(tiles, axis=1)


def _attn_schedule(n_q_tiles):
    far = [(kj, qi, 0) for qi in range(n_q_tiles) for kj in range(qi - 1)]
    near = []
    for qi in range(n_q_tiles):
        if qi >= 1:
            near.append((qi - 1, qi, 1))
        near.append((qi, qi, 0))
    return far, near


def _attn_kernel(sk_ref, sq_ref, q_ref, k_ref, v_ref, bias_ref, qg_ref, kg_ref, lam_ref, sgc_ref,
                 o_ref, kn_ref, vt_ref, qpt1_ref, qpt2_ref, sa1_ref, sa2_ref, sb1_ref, sb2_ref,
                 acc1_ref, acc2_ref, m1_ref, m2_ref, *, lambda_init, n_far, n_steps):
    qpt_refs, acc_refs, m_refs = (qpt1_ref, qpt2_ref), (acc1_ref, acc2_ref), (m1_ref, m2_ref)
    buf_a, buf_b = (sa1_ref, sa2_ref), (sb1_ref, sb2_ref)
    seq = k_ref.shape[1]
    tq = ATTN_TILE
    pr = ATTN_PREP_ROWS
    nt = (((1,), (1,)), ((), ()))

    row_i = lax.broadcasted_iota(I32, (HEAD_W, HEAD_W), 0)
    col_i = lax.broadcasted_iota(I32, (HEAD_W, HEAD_W), 1)

    def onehot(mask):
        return jnp.where(mask, 1.0, 0.0).astype(BF16)

    group_ones = onehot((row_i < HEAD_DIM) == (col_i < HEAD_DIM))
    ident = onehot(row_i == col_i)
    pick1 = onehot((row_i == col_i) & (row_i < HEAD_DIM))
    pick2 = onehot((row_i == col_i) & (row_i >= HEAD_DIM))

    def picked_transpose(pick, x):
        return lax.dot_general(pick, x, nt, preferred_element_type=F32).astype(BF16)

    def half_norm(x, gain):
        sq = x * x
        hi = sq.astype(BF16)
        lo = (sq - hi.astype(F32)).astype(BF16)
        g = (jnp.dot(hi, group_ones, preferred_element_type=F32)
             + jnp.dot(lo, group_ones, preferred_element_type=F32))
        return x * lax.rsqrt(g / HEAD_DIM + EPS) * gain

    def prep(c, carry):
        r0 = pl.multiple_of(c * pr, pr)
        rows = pl.ds(r0, pr)
        kn_ref[rows, :] = half_norm(k_ref[0, rows, :].astype(F32), kg_ref[...]).astype(BF16)
        vt_ref[0:HEAD_W, rows] = picked_transpose(ident, v_ref[0, rows, :])
        qn = (half_norm(q_ref[0, rows, :].astype(F32), qg_ref[...]) * (HEAD_DIM ** -0.5 * LOG2E)).astype(BF16)
        qpt1_ref[:, rows] = picked_transpose(pick1, qn)
        qpt2_ref[:, rows] = picked_transpose(pick2, qn)
        return carry

    lax.fori_loop(0, seq // pr, prep, 0, unroll=4)
    ones_row = lax.broadcasted_iota(I32, (VT_ROWS - HEAD_W, seq), 0) == 0
    vt_ref[HEAD_W:VT_ROWS, :] = jnp.where(ones_row, 1.0, 0.0).astype(BF16)
    for c in range(2):
        m_refs[c][...] = jnp.full(m_refs[c].shape, MASK_VALUE, F32)
        acc_refs[c][...] = jnp.zeros(acc_refs[c].shape, F32)

    def tiles(t):
        return (pl.ds(pl.multiple_of(sk_ref[t] * tq, tq), tq), pl.ds(pl.multiple_of(sq_ref[t] * tq, tq), tq))

    def logits(t, dst):
        keys, queries = tiles(t)
        kt = kn_ref[keys, :]
        for c in range(2):
            dst[c][...] = jnp.dot(kt, qpt_refs[c][:, queries], preferred_element_type=F32)

    def kind(t):
        if t < n_far:
            return FAR
        return DIAG if (t - n_far) % 2 == 0 else SUBDIAG

    corner = (slice(tq - MAX_DISTANCE, tq), slice(0, MAX_DISTANCE))

    def absorb(t, src, step_kind):
        keys, queries = tiles(t)
        vt = vt_ref[:, keys]
        for c in range(2):
            if step_kind == SUBDIAG:
                src[c][corner] = src[c][corner] + bias_ref[(0, 1) + corner]
            s = src[c][...]
            if step_kind == DIAG:
                s = s + bias_ref[0, 0]
            m_old = m_refs[c][:, queries]
            m_new = jnp.maximum(m_old, jnp.max(s, axis=0, keepdims=True))
            m_refs[c][:, queries] = m_new
            p = jnp.exp2(s - m_new).astype(BF16)
            pv = jnp.dot(vt, p, preferred_element_type=F32)
            acc_refs[c][:, queries] = jnp.exp2(m_old - m_new) * acc_refs[c][:, queries] + pv

    def pair(t0, kinds, trailing):
        logits(t0 + 1, buf_b)
        absorb(t0, buf_a, kinds[0])
        if trailing:
            logits(t0 + 2, buf_a)
        absorb(t0 + 1, buf_b, kinds[1])

    def rolled(first, count):
        kinds = (kind(first), kind(first + 1))

        def body(i, carry):
            pair(first + 2 * i, kinds, True)
            return carry
        if count > 0:
            lax.fori_loop(0, count, body, 0)

    logits(0, buf_a)
    rolled(0, n_far // 2)
    t = 2 * (n_far // 2)
    if n_far % 2:
        pair(t, (kind(t), kind(t + 1)), t + 2 < n_steps)
        t += 2
    rest = n_steps - t
    n_rolled = max(rest // 2 - (1 - rest % 2), 0)
    rolled(t, n_rolled)
    t += 2 * n_rolled
    if rest // 2 > n_rolled:
        pair(t, (kind(t), kind(t + 1)), False)
        t += 2
    if rest % 2:
        absorb(t, buf_a, kind(t))

    lp = lam_ref[...]
    lam = (jnp.exp(jnp.sum(lp[0:1] * lp[1:2], axis=-1, keepdims=True))
           - jnp.exp(jnp.sum(lp[2:3] * lp[3:4], axis=-1, keepdims=True)) + lambda_init)
    out_gain = sgc_ref[...] * (1.0 - lambda_init)

    def finish(c, carry):
        queries = pl.ds(pl.multiple_of(c * tq, tq), tq)
        a1 = acc1_ref[:, queries]
        a2 = acc2_ref[:, queries]
        ot = a1[:HEAD_W] / a1[HEAD_W:HEAD_W + 1] - lam * (a2[:HEAD_W] / a2[HEAD_W:HEAD_W + 1])
        ms = jnp.mean(ot * ot, axis=0, keepdims=True)
        ot = ot * lax.rsqrt(ms + EPS) * out_gain
        o_ref[0, queries, :] = ot.T.astype(o_ref.dtype)
        return carry

    lax.fori_loop(0, seq // tq, finish, 0)


def _attention(qkv, bias_tiles, q_gain2, k_gain2, lam_params, subln_gain_col, lambda_init):
    b, s, w3 = qkv.shape
    w = w3 // 3
    n_heads = w // HEAD_W
    assert s % ATTN_TILE == 0
    tq = ATTN_TILE
    far, near = _attn_schedule(s // tq)
    steps = jnp.asarray(far + near, I32)

    def im(f):
        return lambda bi, h, sk, sq: f(bi, h)

    grid_spec = pltpu.PrefetchScalarGridSpec(
        num_scalar_prefetch=2,
        grid=(b, n_heads),
        in_specs=[pl.BlockSpec((1, s, HEAD_W), im(lambda bi, h: (bi, 0, h))),
                  pl.BlockSpec((1, s, HEAD_W), im(lambda bi, h: (bi, 0, n_heads + h))),
                  pl.BlockSpec((1, s, HEAD_W), im(lambda bi, h: (bi, 0, 2 * n_heads + h))),
                  pl.BlockSpec((1, 2, tq, tq), im(lambda bi, h: (h, 0, 0, 0))),
                  pl.BlockSpec((1, HEAD_W), im(lambda bi, h: (0, 0))),
                  pl.BlockSpec((1, HEAD_W), im(lambda bi, h: (0, 0))),
                  pl.BlockSpec((4, HEAD_DIM), im(lambda bi, h: (0, 0))),
                  pl.BlockSpec((HEAD_W, 1), im(lambda bi, h: (0, 0)))],
        out_specs=pl.BlockSpec((1, s, HEAD_W), im(lambda bi, h: (bi, 0, h))),
        scratch_shapes=[pltpu.VMEM((s, HEAD_W), BF16),
                        pltpu.VMEM((VT_ROWS, s), BF16)]
        + [pltpu.VMEM((HEAD_W, s), BF16)] * 2
        + [pltpu.VMEM((tq, tq), F32)] * 4
        + [pltpu.VMEM((VT_ROWS, s), F32)] * 2
        + [pltpu.VMEM((1, s), F32)] * 2,
    )
    return pl.pallas_call(
        functools.partial(_attn_kernel, lambda_init=lambda_init, n_far=len(far), n_steps=len(far) + len(near)),
        grid_spec=grid_spec,
        out_shape=jax.ShapeDtypeStruct((b, s, w), BF16),
        compiler_params=_params("parallel", "parallel"),
        name="diff_attention",
    )(steps[:, 0], steps[:, 1], qkv, qkv, qkv, bias_tiles, q_gain2, k_gain2, lam_params,
      subln_gain_col)


def _pool_kernel(x_ref, g_ref, win_ref, wgrp_ref, wout_ref, scale_ref, o_ref, ext_ref, pooled_ref):
    j = pl.program_id(1)
    tm = x_ref.shape[1]
    d = x_ref.shape[2]
    gw = d // len(POOL_WINDOWS)
    x = x_ref[0]

    @pl.when(j == 0)
    def _():
        ext_ref[0:POOL_HALO, :] = jnp.zeros((POOL_HALO, d), F32)

    @pl.when(j > 0)
    def _():
        ext_ref[0:POOL_HALO, :] = ext_ref[tm:tm + POOL_HALO, :]

    h = _rms(x, g_ref[...]).astype(BF16)
    ext_ref[POOL_HALO:POOL_HALO + tm, :] = jnp.dot(h, win_ref[...], preferred_element_type=F32)

    pos1 = (j * tm + 1 + lax.broadcasted_iota(I32, (tm, 1), 0)).astype(F32)
    for g, win in enumerate(POOL_WINDOWS):
        c0, c1 = g * gw, (g + 1) * gw
        u = ext_ref[POOL_HALO:POOL_HALO + tm, c0:c1]
        s = u
        for k in range(1, win):
            s = s + ext_ref[POOL_HALO - k:POOL_HALO - k + tm, c0:c1]
        inv_cnt = 1.0 / jnp.minimum(pos1, float(win))
        pooled = (s * inv_cnt - u).astype(BF16)
        pooled_ref[:, c0:c1] = jnp.dot(pooled, wgrp_ref[g], preferred_element_type=F32).astype(BF16)

    y = jnp.dot(pooled_ref[...], wout_ref[...], preferred_element_type=F32)
    o_ref[0] = x + y * scale_ref[...]


def _pool_mixer(x, gain, w_in, w_group, w_out, scale):
    b, s, d = x.shape
    tm = min(ROW_TILE, s)
    ng = len(POOL_WINDOWS)
    gw = d // ng
    return pl.pallas_call(
        _pool_kernel,
        grid=(b, s // tm),
        in_specs=[pl.BlockSpec((1, tm, d), lambda bi, j: (bi, j, 0)),
                  pl.BlockSpec((1, d), lambda bi, j: (0, 0)),
                  pl.BlockSpec((d, d), lambda bi, j: (0, 0)),
                  pl.BlockSpec((ng, gw, gw), lambda bi, j: (0, 0, 0)),
                  pl.BlockSpec((d, d), lambda bi, j: (0, 0)),
                  pl.BlockSpec((1, d), lambda bi, j: (0, 0))],
        out_specs=pl.BlockSpec((1, tm, d), lambda bi, j: (bi, j, 0)),
        out_shape=jax.ShapeDtypeStruct((b, s, d), F32),
        scratch_shapes=[pltpu.VMEM((POOL_HALO + tm, d), F32),
                        pltpu.VMEM((tm, d), BF16)],
        compiler_params=_params("arbitrary", "arbitrary"),
        name="pool_mixer",
    )(x, gain, w_in, w_group, w_out, scale)


def _router_kernel(x_ref, g_ref, wr_ref, br_ref, o_ref, cnt_ref, carry_ref):
    tm = x_ref.shape[0]

    @pl.when(pl.program_id(0) == 0)
    def _():
        carry_ref[...] = jnp.zeros(carry_ref.shape, F32)

    h = _rms(x_ref[...], g_ref[...])
    h_hi = h.astype(BF16)
    h_lo = (h - h_hi.astype(F32)).astype(BF16)
    w = wr_ref[...]
    half = ROUTER_COLS // 2
    nt = (((1,), (1,)), ((), ()))
    a = lax.dot_general(w, h_hi, nt, preferred_element_type=F32)
    b = lax.dot_general(w, h_lo, nt, preferred_element_type=F32)
    lt = a[:half] + a[half:] + b[:half] + br_ref[...]

    def row(r):
        return lt[r:r + 1, :]

    g_best = row(N_EXPERTS)
    g_idx = jnp.zeros((1, tm), I32)
    for g in range(1, N_EXPERT_GROUPS):
        v = row(N_EXPERTS + g)
        better = v > g_best
        g_best = jnp.where(better, v, g_best)
        g_idx = jnp.where(better, g, g_idx)
    denom = jnp.zeros((1, tm), F32)
    for g in range(N_EXPERT_GROUPS):
        denom = denom + jnp.exp(row(N_EXPERTS + g) - g_best)
    g_prob = 1.0 / denom

    sel = []
    for e in range(EXPERTS_PER_GROUP):
        v = row(e)
        for g in range(1, N_EXPERT_GROUPS):
            v = jnp.where(g_idx == g, row(g * EXPERTS_PER_GROUP + e), v)
        sel.append(v)

    def top1(vals):
        best, idx = vals[0], jnp.zeros((1, tm), I32)
        for e in range(1, EXPERTS_PER_GROUP):
            better = vals[e] > best
            best = jnp.where(better, vals[e], best)
            idx = jnp.where(better, e, idx)
        return best, idx

    v0, i0 = top1(sel)
    v1, i1 = top1([jnp.where(i0 == e, -jnp.inf, sel[e]) for e in range(EXPERTS_PER_GROUP)])
    t = jnp.exp(v1 - v0)
    w0 = g_prob / (1.0 + t)
    w1 = w0 * t
    e0 = g_idx * EXPERTS_PER_GROUP + i0
    e1 = g_idx * EXPERTS_PER_GROUP + i1

    eid = lax.broadcasted_iota(I32, (N_EXPERTS, tm), 0)
    oh0 = eid == e0
    oh1 = eid == e1
    onehot = jnp.where(oh0 | oh1, 1.0, 0.0)
    earlier = (lax.broadcasted_iota(I32, (tm, tm), 0) < lax.broadcasted_iota(I32, (tm, tm), 1))
    prefix = jnp.dot(onehot.astype(BF16), jnp.where(earlier, 1.0, 0.0).astype(BF16),
                     preferred_element_type=F32)
    before = prefix + carry_ref[:, 0:1]
    rank0 = jnp.sum(jnp.where(oh0, before, 0.0), axis=0, keepdims=True)
    rank1 = jnp.sum(jnp.where(oh1, before, 0.0), axis=0, keepdims=True)
    carry_ref[...] = carry_ref[...] + jnp.sum(onehot, axis=1, keepdims=True)
    cnt_ref[...] = carry_ref[...]

    o_ref[...] = jnp.concatenate(
        [e0.astype(F32), e1.astype(F32), w0, w1, rank0, rank1, jnp.zeros((2, tm), F32)], axis=0)


def _router(x2, gain, wr, br):
    t, d = x2.shape
    tm = min(ROW_TILE, t)
    assert t * TOP_K < 2 ** 24
    return pl.pallas_call(
        _router_kernel,
        grid=(t // tm,),
        in_specs=[pl.BlockSpec((tm, d), lambda i: (i, 0)),
                  pl.BlockSpec((1, d), lambda i: (0, 0)),
                  pl.BlockSpec((ROUTER_COLS, d), lambda i: (0, 0)),
                  pl.BlockSpec((ROUTER_COLS // 2, 1), lambda i: (0, 0))],
        out_specs=[pl.BlockSpec((8, tm), lambda i: (0, i)),
                   pl.BlockSpec((N_EXPERTS, LANES), lambda i: (0, 0))],
        out_shape=[jax.ShapeDtypeStruct((8, t), F32),
                   jax.ShapeDtypeStruct((N_EXPERTS, LANES), F32)],
        scratch_shapes=[pltpu.VMEM((N_EXPERTS, LANES), F32)],
        compiler_params=_params("arbitrary"),
        name="router",
    )(x2, gain, wr, br)


def _slot_positions(route, counts_f, blk, n_blocks):
    e = route[0:TOP_K].astype(I32).T.reshape(-1)
    rank = route[2 * TOP_K:3 * TOP_K].astype(I32).T.reshape(-1)
    counts = counts_f[:, 0].astype(I32)
    padded = (counts + blk - 1) // blk * blk
    pad_end = jnp.cumsum(padded)
    pad_start = pad_end - padded
    pos = rank
    for n in range(N_EXPERTS):
        pos = pos + jnp.where(e == n, pad_start[n], 0)
    first_row = jnp.arange(n_blocks, dtype=I32) * blk
    block_expert = jnp.minimum(
        jnp.sum((pad_end[None, :] <= first_row[:, None]).astype(I32), axis=1), N_EXPERTS - 1)
    n_used = (pad_end[-1] // blk).reshape(1)
    return pos.astype(I32), block_expert.astype(I32), n_used.astype(I32)


def _row_copy(src_ref, src_row, dst_ref, dst_row, sem):
    return pltpu.make_async_copy(src_ref.at[pl.ds(src_row, 1)], dst_ref.at[pl.ds(dst_row, 1)], sem)


def _dispatch_kernel(pos_ref, x_ref, g_ref, xs_in_ref, xs_ref, buf_ref, sem):
    del xs_in_ref
    i = pl.program_id(0)
    n = pl.num_programs(0)
    tm = x_ref.shape[0]
    slot = i % 2

    def wait_slot(sl):
        for _ in range(TOP_K):
            pltpu.make_async_copy(buf_ref.at[sl], xs_ref.at[pl.ds(0, tm)], sem.at[sl]).wait()

    @pl.when(i >= 2)
    def _():
        wait_slot(slot)

    buf_ref[slot] = _pack_rows(_rms(x_ref[...], g_ref[...]))

    def issue(r, c):
        for k in range(TOP_K):
            p = pos_ref[(i * tm + r) * TOP_K + k]
            _row_copy(buf_ref.at[slot], r, xs_ref, p, sem.at[slot]).start()
        return c

    lax.fori_loop(0, tm, issue, 0, unroll=ISSUE_UNROLL)

    @pl.when(i == n - 1)
    def _():
        wait_slot(slot)

    @pl.when((i == n - 1) & (n >= 2))
    def _():
        wait_slot(1 - slot)


def _dispatch(pos, x2, gain, n_slots):
    t, d = x2.shape
    tm = min(MOVE_TILE, t)
    zeros = jnp.zeros((n_slots, d // 2), U32)
    grid_spec = pltpu.PrefetchScalarGridSpec(
        num_scalar_prefetch=1,
        grid=(t // tm,),
        in_specs=[pl.BlockSpec((tm, d), lambda i, pos: (i, 0)),
                  pl.BlockSpec((1, d), lambda i, pos: (0, 0)),
                  pl.BlockSpec(memory_space=pl.ANY)],
        out_specs=pl.BlockSpec(memory_space=pl.ANY),
        scratch_shapes=[pltpu.VMEM((2, tm, d // 2), U32),
                        pltpu.SemaphoreType.DMA((2,))],
    )
    return pl.pallas_call(
        _dispatch_kernel,
        grid_spec=grid_spec,
        out_shape=jax.ShapeDtypeStruct((n_slots, d // 2), U32),
        input_output_aliases={3: 0},
        compiler_params=_params("arbitrary"),
        name="dispatch",
    )(pos, x2, gain, zeros)


def _expert_kernel(be_ref, nu_ref, xs_ref, wg_ref, wu_ref, wd_ref, y_ref, wgb_ref, wub_ref, wdb_ref):
    i = pl.program_id(0)
    e = be_ref[i]
    prev = be_ref[jnp.maximum(i - 1, 0)]

    @pl.when((i == 0) | (e != prev))
    def _():
        wgb_ref[...] = wg_ref[0, 0].astype(BF16)
        wub_ref[...] = wu_ref[0, 0].astype(BF16)
        wdb_ref[...] = wd_ref[0, 0].astype(BF16)

    @pl.when(i < nu_ref[0])
    def _():
        x = _unpack_rows(xs_ref[...]).astype(BF16)
        g = jnp.dot(x, wgb_ref[...], preferred_element_type=F32)
        u = jnp.dot(x, wub_ref[...], preferred_element_type=F32)
        a = (g / (1.0 + jnp.exp(-g)) * u).astype(BF16)
        y_ref[...] = _pack_rows(jnp.dot(a, wdb_ref[...], preferred_element_type=F32))

    @pl.when(i >= nu_ref[0])
    def _():
        y_ref[...] = jnp.zeros(y_ref.shape, U32)


def _experts(block_expert, n_used, xs, w_gate, w_up, w_down, layer):
    n_slots, dh = xs.shape
    blk = EXPERT_BLOCK
    d, ff = w_gate.shape[2], w_gate.shape[3]
    grid_spec = pltpu.PrefetchScalarGridSpec(
        num_scalar_prefetch=2,
        grid=(n_slots // blk,),
        in_specs=[pl.BlockSpec((blk, dh), lambda i, be, nu: (i, 0)),
                  pl.BlockSpec((1, 1, d, ff), lambda i, be, nu: (layer, be[i], 0, 0)),
                  pl.BlockSpec((1, 1, d, ff), lambda i, be, nu: (layer, be[i], 0, 0)),
                  pl.BlockSpec((1, 1, ff, d), lambda i, be, nu: (layer, be[i], 0, 0))],
        out_specs=pl.BlockSpec((blk, dh), lambda i, be, nu: (i, 0)),
        scratch_shapes=[pltpu.VMEM((d, ff), BF16),
                        pltpu.VMEM((d, ff), BF16),
                        pltpu.VMEM((ff, d), BF16)],
    )
    return pl.pallas_call(
        _expert_kernel,
        grid_spec=grid_spec,
        out_shape=jax.ShapeDtypeStruct((n_slots, dh), U32),
        compiler_params=_params("arbitrary"),
        name="experts",
    )(block_expert, n_used, xs, w_gate, w_up, w_down)


def _combine_kernel(pos_ref, y_ref, x_ref, w_ref, o_ref, buf_ref, sem):
    i = pl.program_id(0)
    n = pl.num_programs(0)
    tm = x_ref.shape[0]
    slot = i % 2

    def issue_tile(tile, sl):
        def body(r, c):
            for k in range(TOP_K):
                p = pos_ref[(tile * tm + r) * TOP_K + k]
                _row_copy(y_ref, p, buf_ref.at[sl], k * tm + r, sem.at[sl]).start()
            return c
        lax.fori_loop(0, tm, body, 0, unroll=ISSUE_UNROLL)

    @pl.when(i == 0)
    def _():
        issue_tile(0, 0)

    @pl.when(i + 1 < n)
    def _():
        issue_tile(i + 1, 1 - slot)

    pltpu.make_async_copy(y_ref.at[pl.ds(0, TOP_K * tm)], buf_ref.at[slot], sem.at[slot]).wait()

    rows = buf_ref[slot]
    w = w_ref[...]
    out = x_ref[...]
    for k in range(TOP_K):
        out = out + w[:, k:k + 1] * _unpack_rows(rows[k * tm:(k + 1) * tm])
    o_ref[...] = out


def _combine(pos, y, x2, weights):
    t, d = x2.shape
    tm = min(MOVE_TILE, t)
    grid_spec = pltpu.PrefetchScalarGridSpec(
        num_scalar_prefetch=1,
        grid=(t // tm,),
        in_specs=[pl.BlockSpec(memory_space=pl.ANY),
                  pl.BlockSpec((tm, d), lambda i, pos: (i, 0)),
                  pl.BlockSpec((tm, TOP_K), lambda i, pos: (i, 0))],
        out_specs=pl.BlockSpec((tm, d), lambda i, pos: (i, 0)),
        scratch_shapes=[pltpu.VMEM((2, TOP_K * tm, d // 2), U32),
                        pltpu.SemaphoreType.DMA((2,))],
    )
    return pl.pallas_call(
        _combine_kernel,
        grid_spec=grid_spec,
        out_shape=jax.ShapeDtypeStruct((t, d), F32),
        compiler_params=_params("arbitrary"),
        name="combine",
    )(pos, y, x2, weights)


def _hier_moe(x2, gain, wg1, bg1, wg2, bg2, w_gate, w_up, w_down, layer):
    t, d = x2.shape
    half = ROUTER_COLS // 2
    pad = half - N_EXPERTS - N_EXPERT_GROUPS
    w_f32 = jnp.concatenate([wg2.astype(F32), wg1.astype(F32), jnp.zeros((d, pad), F32)], axis=1)
    w_hi = w_f32.astype(BF16)
    w_lo = (w_f32 - w_hi.astype(F32)).astype(BF16)
    wr = jnp.concatenate([w_hi, w_lo], axis=1).T
    br = jnp.concatenate([bg2.astype(F32), bg1.astype(F32), jnp.zeros((pad,), F32)]).reshape(half, 1)
    route, counts = _router(x2, gain, wr, br)

    blk = EXPERT_BLOCK
    n_blocks = -(-(t * TOP_K) // blk) + N_EXPERTS
    pos, block_expert, n_used = _slot_positions(route, counts, blk, n_blocks)
    xs = _dispatch(pos, x2, gain, n_blocks * blk)
    y = _experts(block_expert, n_used, xs, w_gate, w_up, w_down, layer)
    return _combine(pos, y, x2, route[TOP_K:2 * TOP_K].T)


def kernel(x, rel_bias, attn_norm, w_qkv, q_gain, k_gain, lambda_q1, lambda_k1, lambda_q2, lambda_k2,
           subln_gain, w_o, pool_norm, pool_w_in, pool_w_group, pool_w_out, pool_scale,
           ffn_norm, router_group_w, router_group_b, router_expert_w, router_expert_b,
           w_gate, w_up, w_down):
    b, s, d = x.shape
    depth = ffn_norm.shape[0]
    x = x.astype(F32)
    bias_tiles = _bias_tiles(rel_bias)
    for i in range(depth):
        j = i // N_MIXERS
        if i % N_MIXERS == 0:
            lambda_init = 0.8 - 0.6 * math.exp(-0.3 * i)
            x2 = x.reshape(b * s, d)
            qkv = _norm_matmul(x2, attn_norm[j].reshape(1, d).astype(F32), w_qkv[j].astype(BF16))
            lam_params = jnp.stack([lambda_q1[j], lambda_k1[j], lambda_q2[j], lambda_k2[j]]).astype(F32)
            o = _attention(qkv.reshape(b, s, -1), bias_tiles,
                           jnp.tile(q_gain[j].astype(F32), 2).reshape(1, HEAD_W),
                           jnp.tile(k_gain[j].astype(F32), 2).reshape(1, HEAD_W),
                           lam_params, subln_gain[j].reshape(HEAD_W, 1).astype(F32), lambda_init)
            x2 = _matmul_residual(o.reshape(b * s, -1), w_o[j].astype(BF16), x2)
        else:
            x3 = _pool_mixer(x, pool_norm[j].reshape(1, d).astype(F32), pool_w_in[j].astype(BF16),
                             pool_w_group[j].astype(BF16), pool_w_out[j].astype(BF16),
                             pool_scale[j].reshape(1, d).astype(F32))
            x2 = x3.reshape(b * s, d)
        x2 = _hier_moe(x2, ffn_norm[i].reshape(1, d).astype(F32), router_group_w[i], router_group_b[i],
                       router_expert_w[i], router_expert_b[i], w_gate, w_up, w_down, i)
        x = x2.reshape(b, s, d)
    return x
```

```python
import functools
import math

import jax
import jax.numpy as jnp
from jax import lax
from jax.experimental import pallas as pl
from jax.experimental.pallas import tpu as pltpu

F32 = jnp.float32
BF16 = jnp.bfloat16
U32 = jnp.uint32
I32 = jnp.int32

EPS = 1e-6
HEAD_DIM = 64
HEAD_W = 2 * HEAD_DIM
NUM_BUCKETS = 32
MAX_EXACT = NUM_BUCKETS // 2
MAX_DISTANCE = 128
POOL_WINDOWS = (2, 4, 8, 16)
N_EXPERT_GROUPS = 4
EXPERTS_PER_GROUP = 8
N_EXPERTS = N_EXPERT_GROUPS * EXPERTS_PER_GROUP
TOP_K = 2
N_MIXERS = 2
LOG2E = math.log2(math.e)

LANES = 128
VMEM_LIMIT = 48 * 1024 * 1024
ROW_TILE = 512
ATTN_TILE = 512
ATTN_PREP_ROWS = 256
VT_ROWS = HEAD_W + 16
EXPERT_BLOCK = 512
MOVE_TILE = 256
ISSUE_UNROLL = 16
POOL_HALO = 16
MASK_VALUE = -1e30
FAR, SUBDIAG, DIAG = 0, 1, 2
ROUTER_COLS = LANES


def _params(*sem):
    return pltpu.CompilerParams(dimension_semantics=sem, vmem_limit_bytes=VMEM_LIMIT)


def _rms(x, gain):
    ms = jnp.mean(x * x, axis=-1, keepdims=True)
    return x * lax.rsqrt(ms + EPS) * gain


def _pack_rows(v):
    c = v.shape[1] // 2
    bits = lax.bitcast_convert_type(v.astype(BF16).astype(F32), U32)
    return (bits[:, :c] >> 16) | (bits[:, c:] & jnp.uint32(0xFFFF0000))


def _unpack_rows(w):
    lo = lax.bitcast_convert_type(w << 16, F32)
    hi = lax.bitcast_convert_type(w & jnp.uint32(0xFFFF0000), F32)
    return jnp.concatenate([lo, hi], axis=1)


def _norm_matmul_kernel(x_ref, g_ref, w_ref, o_ref, *, n_chunks):
    h = _rms(x_ref[...], g_ref[...]).astype(BF16)
    cw = o_ref.shape[1] // n_chunks
    for c in range(n_chunks):
        o_ref[:, c * cw:(c + 1) * cw] = jnp.dot(
            h, w_ref[:, c * cw:(c + 1) * cw], preferred_element_type=F32).astype(o_ref.dtype)


def _norm_matmul(x2, gain, w):
    t, d = x2.shape
    n = w.shape[1]
    tm = min(ROW_TILE, t)
    return pl.pallas_call(
        functools.partial(_norm_matmul_kernel, n_chunks=n // d),
        grid=(t // tm,),
        in_specs=[pl.BlockSpec((tm, d), lambda i: (i, 0)),
                  pl.BlockSpec((1, d), lambda i: (0, 0)),
                  pl.BlockSpec((d, n), lambda i: (0, 0))],
        out_specs=pl.BlockSpec((tm, n), lambda i: (i, 0)),
        out_shape=jax.ShapeDtypeStruct((t, n), BF16),
        compiler_params=_params("parallel"),
        name="norm_matmul",
    )(x2, gain, w)


def _matmul_residual_kernel(a_ref, w_ref, x_ref, o_ref):
    o_ref[...] = x_ref[...] + jnp.dot(a_ref[...], w_ref[...], preferred_element_type=F32)


def _matmul_residual(a, w, x2):
    t, d = x2.shape
    k = a.shape[1]
    tm = min(ROW_TILE, t)
    return pl.pallas_call(
        _matmul_residual_kernel,
        grid=(t // tm,),
        in_specs=[pl.BlockSpec((tm, k), lambda i: (i, 0)),
                  pl.BlockSpec((k, d), lambda i: (0, 0)),
                  pl.BlockSpec((tm, d), lambda i: (i, 0))],
        out_specs=pl.BlockSpec((tm, d), lambda i: (i, 0)),
        out_shape=jax.ShapeDtypeStruct((t, d), F32),
        compiler_params=_params("parallel"),
        name="matmul_residual",
    )(a, w, x2)


def _rel_bucket(dist):
    n = jnp.maximum(dist, 0)
    nf = jnp.maximum(n, 1).astype(F32)
    large = MAX_EXACT + (jnp.log(nf / MAX_EXACT) / math.log(MAX_DISTANCE / MAX_EXACT)
                         * (NUM_BUCKETS - MAX_EXACT)).astype(I32)
    large = jnp.minimum(large, NUM_BUCKETS - 1)
    return jnp.where(n < MAX_EXACT, n, large)


def _bias_tiles(rel_bias):
    assert ATTN_TILE >= MAX_DISTANCE
    kj = jnp.arange(ATTN_TILE)[:, None]
    qi = jnp.arange(ATTN_TILE)[None, :]
    table = rel_bias.astype(F32)
    vals = (table - table[NUM_BUCKETS - 1]) * LOG2E
    tiles = []
    for off in (0, ATTN_TILE):
        dist = qi - kj + off
        bucket = _rel_bucket(dist)[None]
        b = jnp.zeros((table.shape[1], ATTN_TILE, ATTN_TILE), F32)
        for n in range(NUM_BUCKETS):
            b = jnp.where(bucket == n, vals[n][:, None, None], b)
        tiles.append(jnp.where((dist >= 0)[None], b, MASK_VALUE))
    return jnp.stack(tiles, axis=1)


def _attn_schedule(n_q_tiles):
    far = [(kj, qi, 0) for qi in range(n_q_tiles) for kj in range(qi - 1)]
    near = []
    for qi in range(n_q_tiles):
        if qi >= 1:
            near.append((qi - 1, qi, 1))
        near.append((qi, qi, 0))
    return far, near


def _attn_kernel(sk_ref, sq_ref, q_ref, k_ref, v_ref, bias_ref, qg_ref, kg_ref, lam_ref, sgc_ref,
                 o_ref, kn_ref, vt_ref, qpt1_ref, qpt2_ref, sa1_ref, sa2_ref, sb1_ref, sb2_ref,
                 acc1_ref, acc2_ref, m1_ref, m2_ref, *, lambda_init, n_far, n_steps):
    qpt_refs, acc_refs, m_refs = (qpt1_ref, qpt2_ref), (acc1_ref, acc2_ref), (m1_ref, m2_ref)
    buf_a, buf_b = (sa1_ref, sa2_ref), (sb1_ref, sb2_ref)
    seq = k_ref.shape[1]
    tq = ATTN_TILE
    pr = ATTN_PREP_ROWS
    nt = (((1,), (1,)), ((), ()))

    row_i = lax.broadcasted_iota(I32, (HEAD_W, HEAD_W), 0)
    col_i = lax.broadcasted_iota(I32, (HEAD_W, HEAD_W), 1)

    def onehot(mask):
        return jnp.where(mask, 1.0, 0.0).astype(BF16)

    group_ones = onehot((row_i < HEAD_DIM) == (col_i < HEAD_DIM))
    ident = onehot(row_i == col_i)
    pick1 = onehot((row_i == col_i) & (row_i < HEAD_DIM))
    pick2 = onehot((row_i == col_i) & (row_i >= HEAD_DIM))

    def picked_transpose(pick, x):
        return lax.dot_general(pick, x, nt, preferred_element_type=F32).astype(BF16)

    def half_norm(x, gain):
        sq = x * x
        hi = sq.astype(BF16)
        lo = (sq - hi.astype(F32)).astype(BF16)
        g = (jnp.dot(hi, group_ones, preferred_element_type=F32)
             + jnp.dot(lo, group_ones, preferred_element_type=F32))
        return x * lax.rsqrt(g / HEAD_DIM + EPS) * gain

    def prep(c, carry):
        r0 = pl.multiple_of(c * pr, pr)
        rows = pl.ds(r0, pr)
        kn_ref[rows, :] = half_norm(k_ref[0, rows, :].astype(F32), kg_ref[...]).astype(BF16)
        vt_ref[0:HEAD_W, rows] = picked_transpose(ident, v_ref[0, rows, :])
        qn = (half_norm(q_ref[0, rows, :].astype(F32), qg_ref[...]) * (HEAD_DIM ** -0.5 * LOG2E)).astype(BF16)
        qpt1_ref[:, rows] = picked_transpose(pick1, qn)
        qpt2_ref[:, rows] = picked_transpose(pick2, qn)
        return carry

    lax.fori_loop(0, seq // pr, prep, 0, unroll=4)
    ones_row = lax.broadcasted_iota(I32, (VT_ROWS - HEAD_W, seq), 0) == 0
    vt_ref[HEAD_W:VT_ROWS, :] = jnp.where(ones_row, 1.0, 0.0).astype(BF16)
    for c in range(2):
        m_refs[c][...] = jnp.full(m_refs[c].shape, MASK_VALUE, F32)
        acc_refs[c][...] = jnp.zeros(acc_refs[c].shape, F32)

    def tiles(t):
        return (pl.ds(pl.multiple_of(sk_ref[t] * tq, tq), tq), pl.ds(pl.multiple_of(sq_ref[t] * tq, tq), tq))

    def logits(t, dst):
        keys, queries = tiles(t)
        kt = kn_ref[keys, :]
        for c in range(2):
            dst[c][...] = jnp.dot(kt, qpt_refs[c][:, queries], preferred_element_type=F32)

    def kind(t):
        if t < n_far:
            return FAR
        return DIAG if (t - n_far) % 2 == 0 else SUBDIAG

    corner = (slice(tq - MAX_DISTANCE, tq), slice(0, MAX_DISTANCE))

    def absorb(t, src, step_kind):
        keys, queries = tiles(t)
        vt = vt_ref[:, keys]
        for c in range(2):
            if step_kind == SUBDIAG:
                src[c][corner] = src[c][corner] + bias_ref[(0, 1) + corner]
            s = src[c][...]
            if step_kind == DIAG:
                s = s + bias_ref[0, 0]
            m_old = m_refs[c][:, queries]
            m_new = jnp.maximum(m_old, jnp.max(s, axis=0, keepdims=True))
            m_refs[c][:, queries] = m_new
            p = jnp.exp2(s - m_new).astype(BF16)
            pv = jnp.dot(vt, p, preferred_element_type=F32)
            acc_refs[c][:, queries] = jnp.exp2(m_old - m_new) * acc_refs[c][:, queries] + pv

    def pair(t0, kinds, trailing):
        logits(t0 + 1, buf_b)
        absorb(t0, buf_a, kinds[0])
        if trailing:
            logits(t0 + 2, buf_a)
        absorb(t0 + 1, buf_b, kinds[1])

    def rolled(first, count):
        kinds = (kind(first), kind(first + 1))

        def body(i, carry):
            pair(first + 2 * i, kinds, True)
            return carry
        if count > 0:
            lax.fori_loop(0, count, body, 0)

    logits(0, buf_a)
    rolled(0, n_far // 2)
    t = 2 * (n_far // 2)
    if n_far % 2:
        pair(t, (kind(t), kind(t + 1)), t + 2 < n_steps)
        t += 2
    rest = n_steps - t
    n_rolled = max(rest // 2 - (1 - rest % 2), 0)
    rolled(t, n_rolled)
    t += 2 * n_rolled
    if rest // 2 > n_rolled:
        pair(t, (kind(t), kind(t + 1)), False)
        t += 2
    if rest % 2:
        absorb(t, buf_a, kind(t))

    lp = lam_ref[...]
    lam = (jnp.exp(jnp.sum(lp[0:1] * lp[1:2], axis=-1, keepdims=True))
           - jnp.exp(jnp.sum(lp[2:3] * lp[3:4], axis=-1, keepdims=True)) + lambda_init)
    out_gain = sgc_ref[...] * (1.0 - lambda_init)

    def finish(c, carry):
        queries = pl.ds(pl.multiple_of(c * tq, tq), tq)
        a1 = acc1_ref[:, queries]
        a2 = acc2_ref[:, queries]
        ot = a1[:HEAD_W] / a1[HEAD_W:HEAD_W + 1] - lam * (a2[:HEAD_W] / a2[HEAD_W:HEAD_W + 1])
        ms = jnp.mean(ot * ot, axis=0, keepdims=True)
        ot = ot * lax.rsqrt(ms + EPS) * out_gain
        o_ref[0, queries, :] = ot.T.astype(o_ref.dtype)
        return carry

    lax.fori_loop(0, seq // tq, finish, 0)


def _attention(qkv, bias_tiles, q_gain2, k_gain2, lam_params, subln_gain_col, lambda_init):
    b, s, w3 = qkv.shape
    w = w3 // 3
    n_heads = w // HEAD_W
    assert s % ATTN_TILE == 0
    tq = ATTN_TILE
    far, near = _attn_schedule(s // tq)
    steps = jnp.asarray(far + near, I32)

    def im(f):
        return lambda bi, h, sk, sq: f(bi, h)

    grid_spec = pltpu.PrefetchScalarGridSpec(
        num_scalar_prefetch=2,
        grid=(b, n_heads),
        in_specs=[pl.BlockSpec((1, s, HEAD_W), im(lambda bi, h: (bi, 0, h))),
                  pl.BlockSpec((1, s, HEAD_W), im(lambda bi, h: (bi, 0, n_heads + h))),
                  pl.BlockSpec((1, s, HEAD_W), im(lambda bi, h: (bi, 0, 2 * n_heads + h))),
                  pl.BlockSpec((1, 2, tq, tq), im(lambda bi, h: (h, 0, 0, 0))),
                  pl.BlockSpec((1, HEAD_W), im(lambda bi, h: (0, 0))),
                  pl.BlockSpec((1, HEAD_W), im(lambda bi, h: (0, 0))),
                  pl.BlockSpec((4, HEAD_DIM), im(lambda bi, h: (0, 0))),
                  pl.BlockSpec((HEAD_W, 1), im(lambda bi, h: (0, 0)))],
        out_specs=pl.BlockSpec((1, s, HEAD_W), im(lambda bi, h: (bi, 0, h))),
        scratch_shapes=[pltpu.VMEM((s, HEAD_W), BF16),
                        pltpu.VMEM((VT_ROWS, s), BF16)]
        + [pltpu.VMEM((HEAD_W, s), BF16)] * 2
        + [pltpu.VMEM((tq, tq), F32)] * 4
        + [pltpu.VMEM((VT_ROWS, s), F32)] * 2
        + [pltpu.VMEM((1, s), F32)] * 2,
    )
    return pl.pallas_call(
        functools.partial(_attn_kernel, lambda_init=lambda_init, n_far=len(far), n_steps=len(far) + len(near)),
        grid_spec=grid_spec,
        out_shape=jax.ShapeDtypeStruct((b, s, w), BF16),
        compiler_params=_params("parallel", "parallel"),
        name="diff_attention",
    )(steps[:, 0], steps[:, 1], qkv, qkv, qkv, bias_tiles, q_gain2, k_gain2, lam_params,
      subln_gain_col)


def _pool_kernel(x_ref, g_ref, win_ref, wgrp_ref, wout_ref, scale_ref, o_ref, ext_ref, pooled_ref):
    j = pl.program_id(1)
    tm = x_ref.shape[1]
    d = x_ref.shape[2]
    gw = d // len(POOL_WINDOWS)
    x = x_ref[0]

    @pl.when(j == 0)
    def _():
        ext_ref[0:POOL_HALO, :] = jnp.zeros((POOL_HALO, d), F32)

    @pl.when(j > 0)
    def _():
        ext_ref[0:POOL_HALO, :] = ext_ref[tm:tm + POOL_HALO, :]

    h = _rms(x, g_ref[...]).astype(BF16)
    ext_ref[POOL_HALO:POOL_HALO + tm, :] = jnp.dot(h, win_ref[...], preferred_element_type=F32)

    pos1 = (j * tm + 1 + lax.broadcasted_iota(I32, (tm, 1), 0)).astype(F32)
    for g, win in enumerate(POOL_WINDOWS):
        c0, c1 = g * gw, (g + 1) * gw
        u = ext_ref[POOL_HALO:POOL_HALO + tm, c0:c1]
        s = u
        for k in range(1, win):
            s = s + ext_ref[POOL_HALO - k:POOL_HALO - k + tm, c0:c1]
        inv_cnt = 1.0 / jnp.minimum(pos1, float(win))
        pooled = (s * inv_cnt - u).astype(BF16)
        pooled_ref[:, c0:c1] = jnp.dot(pooled, wgrp_ref[g], preferred_element_type=F32).astype(BF16)

    y = jnp.dot(pooled_ref[...], wout_ref[...], preferred_element_type=F32)
    o_ref[0] = x + y * scale_ref[...]


def _pool_mixer(x, gain, w_in, w_group, w_out, scale):
    b, s, d = x.shape
    tm = min(ROW_TILE, s)
    ng = len(POOL_WINDOWS)
    gw = d // ng
    return pl.pallas_call(
        _pool_kernel,
        grid=(b, s // tm),
        in_specs=[pl.BlockSpec((1, tm, d), lambda bi, j: (bi, j, 0)),
                  pl.BlockSpec((1, d), lambda bi, j: (0, 0)),
                  pl.BlockSpec((d, d), lambda bi, j: (0, 0)),
                  pl.BlockSpec((ng, gw, gw), lambda bi, j: (0, 0, 0)),
                  pl.BlockSpec((d, d), lambda bi, j: (0, 0)),
                  pl.BlockSpec((1, d), lambda bi, j: (0, 0))],
        out_specs=pl.BlockSpec((1, tm, d), lambda bi, j: (bi, j, 0)),
        out_shape=jax.ShapeDtypeStruct((b, s, d), F32),
        scratch_shapes=[pltpu.VMEM((POOL_HALO + tm, d), F32),
                        pltpu.VMEM((tm, d), BF16)],
        compiler_params=_params("arbitrary", "arbitrary"),
        name="pool_mixer",
    )(x, gain, w_in, w_group, w_out, scale)


def _router_kernel(x_ref, g_ref, wr_ref, br_ref, o_ref, cnt_ref, carry_ref):
    tm = x_ref.shape[0]

    @pl.when(pl.program_id(0) == 0)
    def _():
        carry_ref[...] = jnp.zeros(carry_ref.shape, F32)

    h = _rms(x_ref[...], g_ref[...])
    h_hi = h.astype(BF16)
    h_lo = (h - h_hi.astype(F32)).astype(BF16)
    w = wr_ref[...]
    half = ROUTER_COLS // 2
    nt = (((1,), (1,)), ((), ()))
    a = lax.dot_general(w, h_hi, nt, preferred_element_type=F32)
    b = lax.dot_general(w, h_lo, nt, preferred_element_type=F32)
    lt = a[:half] + a[half:] + b[:half] + br_ref[...]

    def row(r):
        return lt[r:r + 1, :]

    g_best = row(N_EXPERTS)
    g_idx = jnp.zeros((1, tm), I32)
    for g in range(1, N_EXPERT_GROUPS):
        v = row(N_EXPERTS + g)
        better = v > g_best
        g_best = jnp.where(better, v, g_best)
        g_idx = jnp.where(better, g, g_idx)
    denom = jnp.zeros((1, tm), F32)
    for g in range(N_EXPERT_GROUPS):
        denom = denom + jnp.exp(row(N_EXPERTS + g) - g_best)
    g_prob = 1.0 / denom

    sel = []
    for e in range(EXPERTS_PER_GROUP):
        v = row(e)
        for g in range(1, N_EXPERT_GROUPS):
            v = jnp.where(g_idx == g, row(g * EXPERTS_PER_GROUP + e), v)
        sel.append(v)

    def top1(vals):
        best, idx = vals[0], jnp.zeros((1, tm), I32)
        for e in range(1, EXPERTS_PER_GROUP):
            better = vals[e] > best
            best = jnp.where(better, vals[e], best)
            idx = jnp.where(better, e, idx)
        return best, idx

    v0, i0 = top1(sel)
    v1, i1 = top1([jnp.where(i0 == e, -jnp.inf, sel[e]) for e in range(EXPERTS_PER_GROUP)])
    t = jnp.exp(v1 - v0)
    w0 = g_prob / (1.0 + t)
    w1 = w0 * t
    e0 = g_idx * EXPERTS_PER_GROUP + i0
    e1 = g_idx * EXPERTS_PER_GROUP + i1

    eid = lax.broadcasted_iota(I32, (N_EXPERTS, tm), 0)
    oh0 = eid == e0
    oh1 = eid == e1
    onehot = jnp.where(oh0 | oh1, 1.0, 0.0)
    earlier = (lax.broadcasted_iota(I32, (tm, tm), 0) < lax.broadcasted_iota(I32, (tm, tm), 1))
    prefix = jnp.dot(onehot.astype(BF16), jnp.where(earlier, 1.0, 0.0).astype(BF16),
                     preferred_element_type=F32)
    before = prefix + carry_ref[:, 0:1]
    rank0 = jnp.sum(jnp.where(oh0, before, 0.0), axis=0, keepdims=True)
    rank1 = jnp.sum(jnp.where(oh1, before, 0.0), axis=0, keepdims=True)
    carry_ref[...] = carry_ref[...] + jnp.sum(onehot, axis=1, keepdims=True)
    cnt_ref[...] = carry_ref[...]

    o_ref[...] = jnp.concatenate(
        [e0.astype(F32), e1.astype(F32), w0, w1, rank0, rank1, jnp.zeros((2, tm), F32)], axis=0)


def _router(x2, gain, wr, br):
    t, d = x2.shape
    tm = min(ROW_TILE, t)
    assert t * TOP_K < 2 ** 24
    return pl.pallas_call(
        _router_kernel,
        grid=(t // tm,),
        in_specs=[pl.BlockSpec((tm, d), lambda i: (i, 0)),
                  pl.BlockSpec((1, d), lambda i: (0, 0)),
                  pl.BlockSpec((ROUTER_COLS, d), lambda i: (0, 0)),
                  pl.BlockSpec((ROUTER_COLS // 2, 1), lambda i: (0, 0))],
        out_specs=[pl.BlockSpec((8, tm), lambda i: (0, i)),
                   pl.BlockSpec((N_EXPERTS, LANES), lambda i: (0, 0))],
        out_shape=[jax.ShapeDtypeStruct((8, t), F32),
                   jax.ShapeDtypeStruct((N_EXPERTS, LANES), F32)],
        scratch_shapes=[pltpu.VMEM((N_EXPERTS, LANES), F32)],
        compiler_params=_params("arbitrary"),
        name="router",
    )(x2, gain, wr, br)


def _slot_positions(route, counts_f, blk, n_blocks):
    e = route[0:TOP_K].astype(I32).T.reshape(-1)
    rank = route[2 * TOP_K:3 * TOP_K].astype(I32).T.reshape(-1)
    counts = counts_f[:, 0].astype(I32)
    padded = (counts + blk - 1) // blk * blk
    pad_end = jnp.cumsum(padded)
    pad_start = pad_end - padded
    experts = jnp.arange(N_EXPERTS, dtype=I32)
    pos = rank + jnp.sum(jnp.where(e[:, None] == experts[None, :], pad_start[None, :], 0), axis=1)
    first_row = jnp.arange(n_blocks, dtype=I32) * blk
    block_expert = jnp.minimum(
        jnp.sum((pad_end[None, :] <= first_row[:, None]).astype(I32), axis=1), N_EXPERTS - 1)
    n_used = (pad_end[-1] // blk).reshape(1)
    return pos.astype(I32), block_expert.astype(I32), n_used.astype(I32)


def _row_copy(src_ref, src_row, dst_ref, dst_row, sem):
    return pltpu.make_async_copy(src_ref.at[pl.ds(src_row, 1)], dst_ref.at[pl.ds(dst_row, 1)], sem)


def _dispatch_kernel(pos_ref, x_ref, g_ref, xs_in_ref, xs_ref, buf_ref, sem):
    del xs_in_ref
    i = pl.program_id(0)
    n = pl.num_programs(0)
    tm = x_ref.shape[0]
    slot = i % 2

    def wait_slot(sl):
        for _ in range(TOP_K):
            pltpu.make_async_copy(buf_ref.at[sl], xs_ref.at[pl.ds(0, tm)], sem.at[sl]).wait()

    @pl.when(i >= 2)
    def _():
        wait_slot(slot)

    buf_ref[slot] = _pack_rows(_rms(x_ref[...], g_ref[...]))

    def issue(g, c):
        base = pl.multiple_of(g * ISSUE_UNROLL, ISSUE_UNROLL)
        for u in range(ISSUE_UNROLL):
            for k in range(TOP_K):
                p = pos_ref[(i * tm + base + u) * TOP_K + k]
                _row_copy(buf_ref.at[slot], base + u, xs_ref, p, sem.at[slot]).start()
        return c

    lax.fori_loop(0, tm // ISSUE_UNROLL, issue, 0)

    @pl.when(i == n - 1)
    def _():
        wait_slot(slot)

    @pl.when((i == n - 1) & (n >= 2))
    def _():
        wait_slot(1 - slot)


def _dispatch(pos, x2, gain, n_slots):
    t, d = x2.shape
    tm = min(MOVE_TILE, t)
    zeros = jnp.zeros((n_slots, d // 2), U32)
    grid_spec = pltpu.PrefetchScalarGridSpec(
        num_scalar_prefetch=1,
        grid=(t // tm,),
        in_specs=[pl.BlockSpec((tm, d), lambda i, pos: (i, 0)),
                  pl.BlockSpec((1, d), lambda i, pos: (0, 0)),
                  pl.BlockSpec(memory_space=pl.ANY)],
        out_specs=pl.BlockSpec(memory_space=pl.ANY),
        scratch_shapes=[pltpu.VMEM((2, tm, d // 2), U32),
                        pltpu.SemaphoreType.DMA((2,))],
    )
    return pl.pallas_call(
        _dispatch_kernel,
        grid_spec=grid_spec,
        out_shape=jax.ShapeDtypeStruct((n_slots, d // 2), U32),
        input_output_aliases={3: 0},
        compiler_params=_params("arbitrary"),
        name="dispatch",
    )(pos, x2, gain, zeros)


def _expert_kernel(be_ref, nu_ref, xs_ref, wg_ref, wu_ref, wd_ref, y_ref, wgb_ref, wub_ref, wdb_ref):
    i = pl.program_id(0)
    e = be_ref[i]
    prev = be_ref[jnp.maximum(i - 1, 0)]

    @pl.when((i == 0) | (e != prev))
    def _():
        wgb_ref[...] = wg_ref[0, 0].astype(BF16)
        wub_ref[...] = wu_ref[0, 0].astype(BF16)
        wdb_ref[...] = wd_ref[0, 0].astype(BF16)

    @pl.when(i < nu_ref[0])
    def _():
        x = _unpack_rows(xs_ref[...]).astype(BF16)
        g = jnp.dot(x, wgb_ref[...], preferred_element_type=F32)
        u = jnp.dot(x, wub_ref[...], preferred_element_type=F32)
        a = (g / (1.0 + jnp.exp(-g)) * u).astype(BF16)
        y_ref[...] = _pack_rows(jnp.dot(a, wdb_ref[...], preferred_element_type=F32))

    @pl.when(i >= nu_ref[0])
    def _():
        y_ref[...] = jnp.zeros(y_ref.shape, U32)


def _experts(block_expert, n_used, xs, w_gate, w_up, w_down, layer):
    n_slots, dh = xs.shape
    blk = EXPERT_BLOCK
    d, ff = w_gate.shape[2], w_gate.shape[3]
    grid_spec = pltpu.PrefetchScalarGridSpec(
        num_scalar_prefetch=2,
        grid=(n_slots // blk,),
        in_specs=[pl.BlockSpec((blk, dh), lambda i, be, nu: (i, 0)),
                  pl.BlockSpec((1, 1, d, ff), lambda i, be, nu: (layer, be[i], 0, 0)),
                  pl.BlockSpec((1, 1, d, ff), lambda i, be, nu: (layer, be[i], 0, 0)),
                  pl.BlockSpec((1, 1, ff, d), lambda i, be, nu: (layer, be[i], 0, 0))],
        out_specs=pl.BlockSpec((blk, dh), lambda i, be, nu: (i, 0)),
        scratch_shapes=[pltpu.VMEM((d, ff), BF16),
                        pltpu.VMEM((d, ff), BF16),
                        pltpu.VMEM((ff, d), BF16)],
    )
    return pl.pallas_call(
        _expert_kernel,
        grid_spec=grid_spec,
        out_shape=jax.ShapeDtypeStruct((n_slots, dh), U32),
        compiler_params=_params("arbitrary"),
        name="experts",
    )(block_expert, n_used, xs, w_gate, w_up, w_down)


def _combine_kernel(pos_ref, y_ref, x_ref, w_ref, o_ref, buf_ref, sem):
    i = pl.program_id(0)
    n = pl.num_programs(0)
    tm = x_ref.shape[0]
    slot = i % 2

    def issue_tile(tile, sl):
        def body(g, c):
            base = pl.multiple_of(g * ISSUE_UNROLL, ISSUE_UNROLL)
            for u in range(ISSUE_UNROLL):
                for k in range(TOP_K):
                    p = pos_ref[(tile * tm + base + u) * TOP_K + k]
                    _row_copy(y_ref, p, buf_ref.at[sl], k * tm + base + u, sem.at[sl]).start()
            return c
        lax.fori_loop(0, tm // ISSUE_UNROLL, body, 0)

    @pl.when(i == 0)
    def _():
        issue_tile(0, 0)

    @pl.when(i + 1 < n)
    def _():
        issue_tile(i + 1, 1 - slot)

    pltpu.make_async_copy(y_ref.at[pl.ds(0, TOP_K * tm)], buf_ref.at[slot], sem.at[slot]).wait()

    rows = buf_ref[slot]
    w = w_ref[...]
    out = x_ref[...]
    for k in range(TOP_K):
        out = out + w[:, k:k + 1] * _unpack_rows(rows[k * tm:(k + 1) * tm])
    o_ref[...] = out


def _combine(pos, y, x2, weights):
    t, d = x2.shape
    tm = min(MOVE_TILE, t)
    grid_spec = pltpu.PrefetchScalarGridSpec(
        num_scalar_prefetch=1,
        grid=(t // tm,),
        in_specs=[pl.BlockSpec(memory_space=pl.ANY),
                  pl.BlockSpec((tm, d), lambda i, pos: (i, 0)),
                  pl.BlockSpec((tm, TOP_K), lambda i, pos: (i, 0))],
        out_specs=pl.BlockSpec((tm, d), lambda i, pos: (i, 0)),
        scratch_shapes=[pltpu.VMEM((2, TOP_K * tm, d // 2), U32),
                        pltpu.SemaphoreType.DMA((2,))],
    )
    return pl.pallas_call(
        _combine_kernel,
        grid_spec=grid_spec,
        out_shape=jax.ShapeDtypeStruct((t, d), F32),
        compiler_params=_params("arbitrary"),
        name="combine",
    )(pos, y, x2, weights)


def _hier_moe(x2, gain, wg1, bg1, wg2, bg2, w_gate, w_up, w_down, layer):
    t, d = x2.shape
    half = ROUTER_COLS // 2
    pad = half - N_EXPERTS - N_EXPERT_GROUPS
    w_f32 = jnp.concatenate([wg2.astype(F32), wg1.astype(F32), jnp.zeros((d, pad), F32)], axis=1)
    w_hi = w_f32.astype(BF16)
    w_lo = (w_f32 - w_hi.astype(F32)).astype(BF16)
    wr = jnp.concatenate([w_hi, w_lo], axis=1).T
    br = jnp.concatenate([bg2.astype(F32), bg1.astype(F32), jnp.zeros((pad,), F32)]).reshape(half, 1)
    route, counts = _router(x2, gain, wr, br)

    blk = EXPERT_BLOCK
    n_blocks = -(-(t * TOP_K) // blk) + N_EXPERTS
    pos, block_expert, n_used = _slot_positions(route, counts, blk, n_blocks)
    xs = _dispatch(pos, x2, gain, n_blocks * blk)
    y = _experts(block_expert, n_used, xs, w_gate, w_up, w_down, layer)
    return _combine(pos, y, x2, route[TOP_K:2 * TOP_K].T)


def kernel(x, rel_bias, attn_norm, w_qkv, q_gain, k_gain, lambda_q1, lambda_k1, lambda_q2, lambda_k2,
           subln_gain, w_o, pool_norm, pool_w_in, pool_w_group, pool_w_out, pool_scale,
           ffn_norm, router_group_w, router_group_b, router_expert_w, router_expert_b,
           w_gate, w_up, w_down):
    b, s, d = x.shape
    depth = ffn_norm.shape[0]
    x = x.astype(F32)
    bias_tiles = _bias_tiles(rel_bias)
    for i in range(depth):
        j = i // N_MIXERS
        if i % N_MIXERS == 0:
            lambda_init = 0.8 - 0.6 * math.exp(-0.3 * i)
            x2 = x.reshape(b * s, d)
            qkv = _norm_matmul(x2, attn_norm[j].reshape(1, d).astype(F32), w_qkv[j].astype(BF16))
            lam_params = jnp.stack([lambda_q1[j], lambda_k1[j], lambda_q2[j], lambda_k2[j]]).astype(F32)
            o = _attention(qkv.reshape(b, s, -1), bias_tiles,
                           jnp.tile(q_gain[j].astype(F32), 2).reshape(1, HEAD_W),
                           jnp.tile(k_gain[j].astype(F32), 2).reshape(1, HEAD_W),
                           lam_params, subln_gain[j].reshape(HEAD_W, 1).astype(F32), lambda_init)
            x2 = _matmul_residual(o.reshape(b * s, -1), w_o[j].astype(BF16), x2)
        else:
            x3 = _pool_mixer(x, pool_norm[j].reshape(1, d).astype(F32), pool_w_in[j].astype(BF16),
                             pool_w_group[j].astype(BF16), pool_w_out[j].astype(BF16),
                             pool_scale[j].reshape(1, d).astype(F32))
            x2 = x3.reshape(b * s, d)
        x2 = _hier_moe(x2, ffn_norm[i].reshape(1, d).astype(F32), router_group_w[i], router_group_b[i],
                       router_expert_w[i], router_expert_b[i], w_gate, w_up, w_down, i)
        x = x2.reshape(b, s, d)
    return x
```

```python
import functools
import math

import jax
import jax.numpy as jnp
from jax import lax
from jax.experimental import pallas as pl
from jax.experimental.pallas import tpu as pltpu

F32 = jnp.float32
BF16 = jnp.bfloat16
U32 = jnp.uint32
I32 = jnp.int32

EPS = 1e-6
HEAD_DIM = 64
HEAD_W = 2 * HEAD_DIM
NUM_BUCKETS = 32
MAX_EXACT = NUM_BUCKETS // 2
MAX_DISTANCE = 128
POOL_WINDOWS = (2, 4, 8, 16)
N_EXPERT_GROUPS = 4
EXPERTS_PER_GROUP = 8
N_EXPERTS = N_EXPERT_GROUPS * EXPERTS_PER_GROUP
TOP_K = 2
N_MIXERS = 2
LOG2E = math.log2(math.e)

LANES = 128
VMEM_LIMIT = 48 * 1024 * 1024
ROW_TILE = 512
ATTN_TILE = 512
ATTN_PREP_ROWS = 256
VT_ROWS = HEAD_W + 16
EXPERT_BLOCK = 512
MOVE_TILE = 256
ISSUE_UNROLL = 16
POOL_HALO = 16
MASK_VALUE = -1e30
FAR, SUBDIAG, DIAG = 0, 1, 2
ROUTER_COLS = LANES


def _params(*sem):
    return pltpu.CompilerParams(dimension_semantics=sem, vmem_limit_bytes=VMEM_LIMIT)


def _rms(x, gain):
    ms = jnp.mean(x * x, axis=-1, keepdims=True)
    return x * lax.rsqrt(ms + EPS) * gain


def _pack_rows(v):
    c = v.shape[1] // 2
    bits = lax.bitcast_convert_type(v.astype(BF16).astype(F32), U32)
    return (bits[:, :c] >> 16) | (bits[:, c:] & jnp.uint32(0xFFFF0000))


def _unpack_rows(w):
    lo = lax.bitcast_convert_type(w << 16, F32)
    hi = lax.bitcast_convert_type(w & jnp.uint32(0xFFFF0000), F32)
    return jnp.concatenate([lo, hi], axis=1)


def _norm_matmul_kernel(x_ref, g_ref, w_ref, o_ref, *, n_chunks):
    h = _rms(x_ref[...], g_ref[...]).astype(BF16)
    cw = o_ref.shape[1] // n_chunks
    for c in range(n_chunks):
        o_ref[:, c * cw:(c + 1) * cw] = jnp.dot(
            h, w_ref[:, c * cw:(c + 1) * cw], preferred_element_type=F32).astype(o_ref.dtype)


def _norm_matmul(x2, gain, w):
    t, d = x2.shape
    n = w.shape[1]
    tm = min(ROW_TILE, t)
    return pl.pallas_call(
        functools.partial(_norm_matmul_kernel, n_chunks=n // d),
        grid=(t // tm,),
        in_specs=[pl.BlockSpec((tm, d), lambda i: (i, 0)),
                  pl.BlockSpec((1, d), lambda i: (0, 0)),
                  pl.BlockSpec((d, n), lambda i: (0, 0))],
        out_specs=pl.BlockSpec((tm, n), lambda i: (i, 0)),
        out_shape=jax.ShapeDtypeStruct((t, n), BF16),
        compiler_params=_params("parallel"),
        name="norm_matmul",
    )(x2, gain, w)


def _matmul_residual_kernel(a_ref, w_ref, x_ref, o_ref):
    o_ref[...] = x_ref[...] + jnp.dot(a_ref[...], w_ref[...], preferred_element_type=F32)


def _matmul_residual(a, w, x2):
    t, d = x2.shape
    k = a.shape[1]
    tm = min(ROW_TILE, t)
    return pl.pallas_call(
        _matmul_residual_kernel,
        grid=(t // tm,),
        in_specs=[pl.BlockSpec((tm, k), lambda i: (i, 0)),
                  pl.BlockSpec((k, d), lambda i: (0, 0)),
                  pl.BlockSpec((tm, d), lambda i: (i, 0))],
        out_specs=pl.BlockSpec((tm, d), lambda i: (i, 0)),
        out_shape=jax.ShapeDtypeStruct((t, d), F32),
        compiler_params=_params("parallel"),
        name="matmul_residual",
    )(a, w, x2)


def _rel_bucket(dist):
    n = jnp.maximum(dist, 0)
    nf = jnp.maximum(n, 1).astype(F32)
    large = MAX_EXACT + (jnp.log(nf / MAX_EXACT) / math.log(MAX_DISTANCE / MAX_EXACT)
                         * (NUM_BUCKETS - MAX_EXACT)).astype(I32)
    large = jnp.minimum(large, NUM_BUCKETS - 1)
    return jnp.where(n < MAX_EXACT, n, large)


def _bias_tiles(rel_bias):
    assert ATTN_TILE >= MAX_DISTANCE
    kj = jnp.arange(ATTN_TILE)[:, None]
    qi = jnp.arange(ATTN_TILE)[None, :]
    table = rel_bias.astype(F32)
    vals = (table - table[NUM_BUCKETS - 1]) * LOG2E
    tiles = []
    for off in (0, ATTN_TILE):
        dist = qi - kj + off
        bucket = _rel_bucket(dist)[None]
        b = jnp.zeros((table.shape[1], ATTN_TILE, ATTN_TILE), F32)
        for n in range(NUM_BUCKETS):
            b = jnp.where(bucket == n, vals[n][:, None, None], b)
        tiles.append(jnp.where((dist >= 0)[None], b, MASK_VALUE))
    return jnp.stack(tiles, axis=1)


def _attn_schedule(n_q_tiles):
    far = [(kj, qi, 0) for qi in range(n_q_tiles) for kj in range(qi - 1)]
    near = []
    for qi in range(n_q_tiles):
        if qi >= 1:
            near.append((qi - 1, qi, 1))
        near.append((qi, qi, 0))
    return far, near


def _attn_kernel(sk_ref, sq_ref, q_ref, k_ref, v_ref, bias_ref, qgc_ref, kgc_ref, lam_ref, sgc_ref,
                 o_ref, kn_ref, vt_ref, qpt1_ref, qpt2_ref, sa1_ref, sa2_ref, sb1_ref, sb2_ref,
                 acc1_ref, acc2_ref, m1_ref, m2_ref, *, lambda_init, n_far, n_steps):
    qpt_refs, acc_refs, m_refs = (qpt1_ref, qpt2_ref), (acc1_ref, acc2_ref), (m1_ref, m2_ref)
    buf_a, buf_b = (sa1_ref, sa2_ref), (sb1_ref, sb2_ref)
    seq = k_ref.shape[1]
    tq = ATTN_TILE
    pr = ATTN_PREP_ROWS
    nt = (((1,), (1,)), ((), ()))

    def identity(n):
        return jnp.where(lax.broadcasted_iota(I32, (n, n), 0) == lax.broadcasted_iota(I32, (n, n), 1),
                         1.0, 0.0).astype(BF16)

    eye_w, eye_rows = identity(HEAD_W), identity(pr)
    first_rows = lax.broadcasted_iota(I32, (HEAD_W, 1), 0) < HEAD_DIM

    def transposed(eye, x):
        return lax.dot_general(eye, x, nt, preferred_element_type=F32)

    def half_norm_t(xt, gain_col):
        sq = xt * xt
        r1 = lax.rsqrt(jnp.sum(sq[:HEAD_DIM], axis=0, keepdims=True) / HEAD_DIM + EPS)
        r2 = lax.rsqrt(jnp.sum(sq[HEAD_DIM:], axis=0, keepdims=True) / HEAD_DIM + EPS)
        return xt * jnp.where(first_rows, r1, r2) * gain_col

    def prep(c, carry):
        r0 = pl.multiple_of(c * pr, pr)
        rows = pl.ds(r0, pr)
        vt_ref[0:HEAD_W, rows] = transposed(eye_w, v_ref[0, rows, :]).astype(BF16)
        knt = half_norm_t(transposed(eye_w, k_ref[0, rows, :]), kgc_ref[...]).astype(BF16)
        kn_ref[rows, :] = transposed(eye_rows, knt).astype(BF16)
        qnt = half_norm_t(transposed(eye_w, q_ref[0, rows, :]), qgc_ref[...]) * (HEAD_DIM ** -0.5 * LOG2E)
        qpt1_ref[:, rows] = jnp.where(first_rows, qnt, 0.0).astype(BF16)
        qpt2_ref[:, rows] = jnp.where(first_rows, 0.0, qnt).astype(BF16)
        return carry

    lax.fori_loop(0, seq // pr, prep, 0, unroll=4)
    ones_row = lax.broadcasted_iota(I32, (VT_ROWS - HEAD_W, seq), 0) == 0
    vt_ref[HEAD_W:VT_ROWS, :] = jnp.where(ones_row, 1.0, 0.0).astype(BF16)
    for c in range(2):
        m_refs[c][...] = jnp.full(m_refs[c].shape, MASK_VALUE, F32)
        acc_refs[c][...] = jnp.zeros(acc_refs[c].shape, F32)

    def tiles(t):
        return (pl.ds(pl.multiple_of(sk_ref[t] * tq, tq), tq), pl.ds(pl.multiple_of(sq_ref[t] * tq, tq), tq))

    def logits(t, dst):
        keys, queries = tiles(t)
        kt = kn_ref[keys, :]
        for c in range(2):
            dst[c][...] = jnp.dot(kt, qpt_refs[c][:, queries], preferred_element_type=F32)

    def kind(t):
        if t < n_far:
            return FAR
        return DIAG if (t - n_far) % 2 == 0 else SUBDIAG

    corner = (slice(tq - MAX_DISTANCE, tq), slice(0, MAX_DISTANCE))

    def absorb(t, src, step_kind):
        keys, queries = tiles(t)
        vt = vt_ref[:, keys]
        for c in range(2):
            if step_kind == SUBDIAG:
                src[c][corner] = src[c][corner] + bias_ref[(0, 1) + corner]
            s = src[c][...]
            if step_kind == DIAG:
                s = s + bias_ref[0, 0]
            m_old = m_refs[c][:, queries]
            m_new = jnp.maximum(m_old, jnp.max(s, axis=0, keepdims=True))
            m_refs[c][:, queries] = m_new
            p = jnp.exp2(s - m_new).astype(BF16)
            pv = jnp.dot(vt, p, preferred_element_type=F32)
            acc_refs[c][:, queries] = jnp.exp2(m_old - m_new) * acc_refs[c][:, queries] + pv

    def pair(t0, kinds, trailing):
        logits(t0 + 1, buf_b)
        absorb(t0, buf_a, kinds[0])
        if trailing:
            logits(t0 + 2, buf_a)
        absorb(t0 + 1, buf_b, kinds[1])

    def rolled(first, count):
        kinds = (kind(first), kind(first + 1))

        def body(i, carry):
            pair(first + 2 * i, kinds, True)
            return carry
        if count > 0:
            lax.fori_loop(0, count, body, 0)

    logits(0, buf_a)
    rolled(0, n_far // 2)
    t = 2 * (n_far // 2)
    if n_far % 2:
        pair(t, (kind(t), kind(t + 1)), t + 2 < n_steps)
        t += 2
    rest = n_steps - t
    n_rolled = max(rest // 2 - (1 - rest % 2), 0)
    rolled(t, n_rolled)
    t += 2 * n_rolled
    if rest // 2 > n_rolled:
        pair(t, (kind(t), kind(t + 1)), False)
        t += 2
    if rest % 2:
        absorb(t, buf_a, kind(t))

    lp = lam_ref[...]
    lam = (jnp.exp(jnp.sum(lp[0:1] * lp[1:2], axis=-1, keepdims=True))
           - jnp.exp(jnp.sum(lp[2:3] * lp[3:4], axis=-1, keepdims=True)) + lambda_init)
    out_gain = sgc_ref[...] * (1.0 - lambda_init)

    def finish(c, carry):
        queries = pl.ds(pl.multiple_of(c * tq, tq), tq)
        a1 = acc1_ref[:, queries]
        a2 = acc2_ref[:, queries]
        ot = a1[:HEAD_W] / a1[HEAD_W:HEAD_W + 1] - lam * (a2[:HEAD_W] / a2[HEAD_W:HEAD_W + 1])
        ms = jnp.mean(ot * ot, axis=0, keepdims=True)
        ot = ot * lax.rsqrt(ms + EPS) * out_gain
        o_ref[0, queries, :] = ot.T.astype(o_ref.dtype)
        return carry

    lax.fori_loop(0, seq // tq, finish, 0)


def _attention(qkv, bias_tiles, q_gain2, k_gain2, lam_params, subln_gain_col, lambda_init):
    b, s, w3 = qkv.shape
    w = w3 // 3
    n_heads = w // HEAD_W
    assert s % ATTN_TILE == 0
    tq = ATTN_TILE
    far, near = _attn_schedule(s // tq)
    steps = jnp.asarray(far + near, I32)

    def im(f):
        return lambda bi, h, sk, sq: f(bi, h)

    grid_spec = pltpu.PrefetchScalarGridSpec(
        num_scalar_prefetch=2,
        grid=(b, n_heads),
        in_specs=[pl.BlockSpec((1, s, HEAD_W), im(lambda bi, h: (bi, 0, h))),
                  pl.BlockSpec((1, s, HEAD_W), im(lambda bi, h: (bi, 0, n_heads + h))),
                  pl.BlockSpec((1, s, HEAD_W), im(lambda bi, h: (bi, 0, 2 * n_heads + h))),
                  pl.BlockSpec((1, 2, tq, tq), im(lambda bi, h: (h, 0, 0, 0))),
                  pl.BlockSpec((HEAD_W, 1), im(lambda bi, h: (0, 0))),
                  pl.BlockSpec((HEAD_W, 1), im(lambda bi, h: (0, 0))),
                  pl.BlockSpec((4, HEAD_DIM), im(lambda bi, h: (0, 0))),
                  pl.BlockSpec((HEAD_W, 1), im(lambda bi, h: (0, 0)))],
        out_specs=pl.BlockSpec((1, s, HEAD_W), im(lambda bi, h: (bi, 0, h))),
        scratch_shapes=[pltpu.VMEM((s, HEAD_W), BF16),
                        pltpu.VMEM((VT_ROWS, s), BF16)]
        + [pltpu.VMEM((HEAD_W, s), BF16)] * 2
        + [pltpu.VMEM((tq, tq), F32)] * 4
        + [pltpu.VMEM((VT_ROWS, s), F32)] * 2
        + [pltpu.VMEM((1, s), F32)] * 2,
    )
    return pl.pallas_call(
        functools.partial(_attn_kernel, lambda_init=lambda_init, n_far=len(far), n_steps=len(far) + len(near)),
        grid_spec=grid_spec,
        out_shape=jax.ShapeDtypeStruct((b, s, w), BF16),
        compiler_params=_params("parallel", "parallel"),
        name="diff_attention",
    )(steps[:, 0], steps[:, 1], qkv, qkv, qkv, bias_tiles, q_gain2, k_gain2, lam_params,
      subln_gain_col)


def _pool_kernel(x_ref, g_ref, win_ref, wgrp_ref, wout_ref, scale_ref, o_ref, ext_ref, pooled_ref):
    j = pl.program_id(1)
    tm = x_ref.shape[1]
    d = x_ref.shape[2]
    gw = d // len(POOL_WINDOWS)
    x = x_ref[0]

    @pl.when(j == 0)
    def _():
        ext_ref[0:POOL_HALO, :] = jnp.zeros((POOL_HALO, d), F32)

    @pl.when(j > 0)
    def _():
        ext_ref[0:POOL_HALO, :] = ext_ref[tm:tm + POOL_HALO, :]

    h = _rms(x, g_ref[...]).astype(BF16)
    ext_ref[POOL_HALO:POOL_HALO + tm, :] = jnp.dot(h, win_ref[...], preferred_element_type=F32)

    pos1 = (j * tm + 1 + lax.broadcasted_iota(I32, (tm, 1), 0)).astype(F32)
    for g, win in enumerate(POOL_WINDOWS):
        c0, c1 = g * gw, (g + 1) * gw
        u = ext_ref[POOL_HALO:POOL_HALO + tm, c0:c1]
        s = u
        for k in range(1, win):
            s = s + ext_ref[POOL_HALO - k:POOL_HALO - k + tm, c0:c1]
        inv_cnt = 1.0 / jnp.minimum(pos1, float(win))
        pooled = (s * inv_cnt - u).astype(BF16)
        pooled_ref[:, c0:c1] = jnp.dot(pooled, wgrp_ref[g], preferred_element_type=F32).astype(BF16)

    y = jnp.dot(pooled_ref[...], wout_ref[...], preferred_element_type=F32)
    o_ref[0] = x + y * scale_ref[...]


def _pool_mixer(x, gain, w_in, w_group, w_out, scale):
    b, s, d = x.shape
    tm = min(ROW_TILE, s)
    ng = len(POOL_WINDOWS)
    gw = d // ng
    return pl.pallas_call(
        _pool_kernel,
        grid=(b, s // tm),
        in_specs=[pl.BlockSpec((1, tm, d), lambda bi, j: (bi, j, 0)),
                  pl.BlockSpec((1, d), lambda bi, j: (0, 0)),
                  pl.BlockSpec((d, d), lambda bi, j: (0, 0)),
                  pl.BlockSpec((ng, gw, gw), lambda bi, j: (0, 0, 0)),
                  pl.BlockSpec((d, d), lambda bi, j: (0, 0)),
                  pl.BlockSpec((1, d), lambda bi, j: (0, 0))],
        out_specs=pl.BlockSpec((1, tm, d), lambda bi, j: (bi, j, 0)),
        out_shape=jax.ShapeDtypeStruct((b, s, d), F32),
        scratch_shapes=[pltpu.VMEM((POOL_HALO + tm, d), F32),
                        pltpu.VMEM((tm, d), BF16)],
        compiler_params=_params("arbitrary", "arbitrary"),
        name="pool_mixer",
    )(x, gain, w_in, w_group, w_out, scale)


def _router_kernel(x_ref, g_ref, wr_ref, br_ref, o_ref, cnt_ref, carry_ref):
    tm = x_ref.shape[0]

    @pl.when(pl.program_id(0) == 0)
    def _():
        carry_ref[...] = jnp.zeros(carry_ref.shape, F32)

    h = _rms(x_ref[...], g_ref[...])
    h_hi = h.astype(BF16)
    h_lo = (h - h_hi.astype(F32)).astype(BF16)
    w = wr_ref[...]
    half = ROUTER_COLS // 2
    nt = (((1,), (1,)), ((), ()))
    a = lax.dot_general(w, h_hi, nt, preferred_element_type=F32)
    b = lax.dot_general(w, h_lo, nt, preferred_element_type=F32)
    lt = a[:half] + a[half:] + b[:half] + br_ref[...]

    def row(r):
        return lt[r:r + 1, :]

    g_best = row(N_EXPERTS)
    g_idx = jnp.zeros((1, tm), I32)
    for g in range(1, N_EXPERT_GROUPS):
        v = row(N_EXPERTS + g)
        better = v > g_best
        g_best = jnp.where(better, v, g_best)
        g_idx = jnp.where(better, g, g_idx)
    denom = jnp.zeros((1, tm), F32)
    for g in range(N_EXPERT_GROUPS):
        denom = denom + jnp.exp(row(N_EXPERTS + g) - g_best)
    g_prob = 1.0 / denom

    sel = []
    for e in range(EXPERTS_PER_GROUP):
        v = row(e)
        for g in range(1, N_EXPERT_GROUPS):
            v = jnp.where(g_idx == g, row(g * EXPERTS_PER_GROUP + e), v)
        sel.append(v)

    def top1(vals):
        best, idx = vals[0], jnp.zeros((1, tm), I32)
        for e in range(1, EXPERTS_PER_GROUP):
            better = vals[e] > best
            best = jnp.where(better, vals[e], best)
            idx = jnp.where(better, e, idx)
        return best, idx

    v0, i0 = top1(sel)
    v1, i1 = top1([jnp.where(i0 == e, -jnp.inf, sel[e]) for e in range(EXPERTS_PER_GROUP)])
    t = jnp.exp(v1 - v0)
    w0 = g_prob / (1.0 + t)
    w1 = w0 * t
    e0 = g_idx * EXPERTS_PER_GROUP + i0
    e1 = g_idx * EXPERTS_PER_GROUP + i1

    eid = lax.broadcasted_iota(I32, (N_EXPERTS, tm), 0)
    oh0 = eid == e0
    oh1 = eid == e1
    onehot = jnp.where(oh0 | oh1, 1.0, 0.0)
    earlier = (lax.broadcasted_iota(I32, (tm, tm), 0) < lax.broadcasted_iota(I32, (tm, tm), 1))
    prefix = jnp.dot(onehot.astype(BF16), jnp.where(earlier, 1.0, 0.0).astype(BF16),
                     preferred_element_type=F32)
    before = prefix + carry_ref[:, 0:1]
    rank0 = jnp.sum(jnp.where(oh0, before, 0.0), axis=0, keepdims=True)
    rank1 = jnp.sum(jnp.where(oh1, before, 0.0), axis=0, keepdims=True)
    carry_ref[...] = carry_ref[...] + jnp.sum(onehot, axis=1, keepdims=True)
    cnt_ref[...] = carry_ref[...]

    o_ref[...] = jnp.concatenate(
        [e0.astype(F32), e1.astype(F32), w0, w1, rank0, rank1, jnp.zeros((2, tm), F32)], axis=0)


def _router(x2, gain, wr, br):
    t, d = x2.shape
    tm = min(ROW_TILE, t)
    assert t * TOP_K < 2 ** 24
    return pl.pallas_call(
        _router_kernel,
        grid=(t // tm,),
        in_specs=[pl.BlockSpec((tm, d), lambda i: (i, 0)),
                  pl.BlockSpec((1, d), lambda i: (0, 0)),
                  pl.BlockSpec((ROUTER_COLS, d), lambda i: (0, 0)),
                  pl.BlockSpec((ROUTER_COLS // 2, 1), lambda i: (0, 0))],
        out_specs=[pl.BlockSpec((8, tm), lambda i: (0, i)),
                   pl.BlockSpec((N_EXPERTS, LANES), lambda i: (0, 0))],
        out_shape=[jax.ShapeDtypeStruct((8, t), F32),
                   jax.ShapeDtypeStruct((N_EXPERTS, LANES), F32)],
        scratch_shapes=[pltpu.VMEM((N_EXPERTS, LANES), F32)],
        compiler_params=_params("arbitrary"),
        name="router",
    )(x2, gain, wr, br)


def _slot_positions(route, counts_f, blk, n_blocks):
    e = route[0:TOP_K].astype(I32).T.reshape(-1)
    rank = route[2 * TOP_K:3 * TOP_K].astype(I32).T.reshape(-1)
    counts = counts_f[:, 0].astype(I32)
    padded = (counts + blk - 1) // blk * blk
    pad_end = jnp.cumsum(padded)
    pad_start = pad_end - padded
    experts = jnp.arange(N_EXPERTS, dtype=I32)
    pos = rank + jnp.sum(jnp.where(e[:, None] == experts[None, :], pad_start[None, :], 0), axis=1)
    first_row = jnp.arange(n_blocks, dtype=I32) * blk
    block_expert = jnp.minimum(
        jnp.sum((pad_end[None, :] <= first_row[:, None]).astype(I32), axis=1), N_EXPERTS - 1)
    n_used = (pad_end[-1] // blk).reshape(1)
    return pos.astype(I32), block_expert.astype(I32), n_used.astype(I32)


def _row_copy(src_ref, src_row, dst_ref, dst_row, sem):
    return pltpu.make_async_copy(src_ref.at[src_row], dst_ref.at[dst_row], sem)


def _dispatch_kernel(pos_ref, x_ref, g_ref, xs_in_ref, xs_ref, buf_ref, sem):
    del xs_in_ref
    i = pl.program_id(0)
    n = pl.num_programs(0)
    tm = x_ref.shape[0]
    slot = i % 2

    def wait_slot(sl):
        for _ in range(TOP_K):
            pltpu.make_async_copy(buf_ref.at[sl], xs_ref.at[pl.ds(0, tm)], sem.at[sl]).wait()

    @pl.when(i >= 2)
    def _():
        wait_slot(slot)

    packed = _pack_rows(_rms(x_ref[...], g_ref[...]))
    buf_ref[slot] = packed.reshape(tm, packed.shape[1] // LANES, LANES)

    def issue(g, c):
        base = g * ISSUE_UNROLL
        for u in range(ISSUE_UNROLL):
            for k in range(TOP_K):
                p = pos_ref[(i * tm + base + u) * TOP_K + k]
                _row_copy(buf_ref.at[slot], base + u, xs_ref, p, sem.at[slot]).start()
        return c

    lax.fori_loop(0, tm // ISSUE_UNROLL, issue, 0)

    @pl.when(i == n - 1)
    def _():
        wait_slot(slot)

    @pl.when((i == n - 1) & (n >= 2))
    def _():
        wait_slot(1 - slot)


def _dispatch(pos, x2, gain, n_slots):
    t, d = x2.shape
    tm = min(MOVE_TILE, t)
    chunks = d // 2 // LANES
    zeros = jnp.zeros((n_slots, chunks, LANES), U32)
    grid_spec = pltpu.PrefetchScalarGridSpec(
        num_scalar_prefetch=1,
        grid=(t // tm,),
        in_specs=[pl.BlockSpec((tm, d), lambda i, pos: (i, 0)),
                  pl.BlockSpec((1, d), lambda i, pos: (0, 0)),
                  pl.BlockSpec(memory_space=pl.ANY)],
        out_specs=pl.BlockSpec(memory_space=pl.ANY),
        scratch_shapes=[pltpu.VMEM((2, tm, chunks, LANES), U32),
                        pltpu.SemaphoreType.DMA((2,))],
    )
    slabs = pl.pallas_call(
        _dispatch_kernel,
        grid_spec=grid_spec,
        out_shape=jax.ShapeDtypeStruct((n_slots, chunks, LANES), U32),
        input_output_aliases={3: 0},
        compiler_params=_params("arbitrary"),
        name="dispatch",
    )(pos, x2, gain, zeros)
    return slabs


def _expert_kernel(be_ref, nu_ref, xs_ref, wg_ref, wu_ref, wd_ref, y_ref, wgb_ref, wub_ref, wdb_ref):
    i = pl.program_id(0)
    e = be_ref[i]
    prev = be_ref[jnp.maximum(i - 1, 0)]

    @pl.when((i == 0) | (e != prev))
    def _():
        wgb_ref[...] = wg_ref[0, 0].astype(BF16)
        wub_ref[...] = wu_ref[0, 0].astype(BF16)
        wdb_ref[...] = wd_ref[0, 0].astype(BF16)

    @pl.when(i < nu_ref[0])
    def _():
        blk, chunks, lanes = xs_ref.shape
        x = _unpack_rows(xs_ref[...].reshape(blk, chunks * lanes)).astype(BF16)
        g = jnp.dot(x, wgb_ref[...], preferred_element_type=F32)
        u = jnp.dot(x, wub_ref[...], preferred_element_type=F32)
        a = (g / (1.0 + jnp.exp(-g)) * u).astype(BF16)
        y = _pack_rows(jnp.dot(a, wdb_ref[...], preferred_element_type=F32))
        y_ref[...] = y.reshape(blk, chunks, lanes)

    @pl.when(i >= nu_ref[0])
    def _():
        y_ref[...] = jnp.zeros(y_ref.shape, U32)


def _experts(block_expert, n_used, xs, w_gate, w_up, w_down, layer):
    n_slots, chunks, lanes = xs.shape
    blk = EXPERT_BLOCK
    d, ff = w_gate.shape[2], w_gate.shape[3]
    grid_spec = pltpu.PrefetchScalarGridSpec(
        num_scalar_prefetch=2,
        grid=(n_slots // blk,),
        in_specs=[pl.BlockSpec((blk, chunks, lanes), lambda i, be, nu: (i, 0, 0)),
                  pl.BlockSpec((1, 1, d, ff), lambda i, be, nu: (layer, be[i], 0, 0)),
                  pl.BlockSpec((1, 1, d, ff), lambda i, be, nu: (layer, be[i], 0, 0)),
                  pl.BlockSpec((1, 1, ff, d), lambda i, be, nu: (layer, be[i], 0, 0))],
        out_specs=pl.BlockSpec((blk, chunks, lanes), lambda i, be, nu: (i, 0, 0)),
        scratch_shapes=[pltpu.VMEM((d, ff), BF16),
                        pltpu.VMEM((d, ff), BF16),
                        pltpu.VMEM((ff, d), BF16)],
    )
    return pl.pallas_call(
        _expert_kernel,
        grid_spec=grid_spec,
        out_shape=jax.ShapeDtypeStruct((n_slots, chunks, lanes), U32),
        compiler_params=_params("arbitrary"),
        name="experts",
    )(block_expert, n_used, xs, w_gate, w_up, w_down)


def _combine_kernel(pos_ref, y_ref, x_ref, w_ref, o_ref, buf_ref, sem):
    i = pl.program_id(0)
    n = pl.num_programs(0)
    tm = x_ref.shape[0]
    slot = i % 2

    def issue_tile(tile, sl):
        def body(g, c):
            base = pl.multiple_of(g * ISSUE_UNROLL, ISSUE_UNROLL)
            for u in range(ISSUE_UNROLL):
                for k in range(TOP_K):
                    p = pos_ref[(tile * tm + base + u) * TOP_K + k]
                    _row_copy(y_ref, p, buf_ref.at[sl], k * tm + base + u, sem.at[sl]).start()
            return c
        lax.fori_loop(0, tm // ISSUE_UNROLL, body, 0)

    @pl.when(i == 0)
    def _():
        issue_tile(0, 0)

    @pl.when(i + 1 < n)
    def _():
        issue_tile(i + 1, 1 - slot)

    pltpu.make_async_copy(y_ref.at[pl.ds(0, TOP_K * tm)], buf_ref.at[slot], sem.at[slot]).wait()

    slabs = buf_ref[slot]
    rows = slabs.reshape(slabs.shape[0], slabs.shape[1] * slabs.shape[2])
    w = w_ref[...]
    out = x_ref[...]
    for k in range(TOP_K):
        out = out + w[:, k:k + 1] * _unpack_rows(rows[k * tm:(k + 1) * tm])
    o_ref[...] = out


def _combine(pos, y, x2, weights):
    t, d = x2.shape
    tm = min(MOVE_TILE, t)
    chunks = d // 2 // LANES
    grid_spec = pltpu.PrefetchScalarGridSpec(
        num_scalar_prefetch=1,
        grid=(t // tm,),
        in_specs=[pl.BlockSpec(memory_space=pl.ANY),
                  pl.BlockSpec((tm, d), lambda i, pos: (i, 0)),
                  pl.BlockSpec((tm, TOP_K), lambda i, pos: (i, 0))],
        out_specs=pl.BlockSpec((tm, d), lambda i, pos: (i, 0)),
        scratch_shapes=[pltpu.VMEM((2, TOP_K * tm, chunks, LANES), U32),
                        pltpu.SemaphoreType.DMA((2,))],
    )
    return pl.pallas_call(
        _combine_kernel,
        grid_spec=grid_spec,
        out_shape=jax.ShapeDtypeStruct((t, d), F32),
        compiler_params=_params("arbitrary"),
        name="combine",
    )(pos, y, x2, weights)


def _hier_moe(x2, gain, wg1, bg1, wg2, bg2, w_gate, w_up, w_down, layer):
    t, d = x2.shape
    half = ROUTER_COLS // 2
    pad = half - N_EXPERTS - N_EXPERT_GROUPS
    w_f32 = jnp.concatenate([wg2.astype(F32), wg1.astype(F32), jnp.zeros((d, pad), F32)], axis=1)
    w_hi = w_f32.astype(BF16)
    w_lo = (w_f32 - w_hi.astype(F32)).astype(BF16)
    wr = jnp.concatenate([w_hi, w_lo], axis=1).T
    br = jnp.concatenate([bg2.astype(F32), bg1.astype(F32), jnp.zeros((pad,), F32)]).reshape(half, 1)
    route, counts = _router(x2, gain, wr, br)

    blk = EXPERT_BLOCK
    n_blocks = -(-(t * TOP_K) // blk) + N_EXPERTS
    pos, block_expert, n_used = _slot_positions(route, counts, blk, n_blocks)
    xs = _dispatch(pos, x2, gain, n_blocks * blk)
    y = _experts(block_expert, n_used, xs, w_gate, w_up, w_down, layer)
    return _combine(pos, y, x2, route[TOP_K:2 * TOP_K].T)


def kernel(x, rel_bias, attn_norm, w_qkv, q_gain, k_gain, lambda_q1, lambda_k1, lambda_q2, lambda_k2,
           subln_gain, w_o, pool_norm, pool_w_in, pool_w_group, pool_w_out, pool_scale,
           ffn_norm, router_group_w, router_group_b, router_expert_w, router_expert_b,
           w_gate, w_up, w_down):
    b, s, d = x.shape
    depth = ffn_norm.shape[0]
    x = x.astype(F32)
    bias_tiles = _bias_tiles(rel_bias)
    for i in range(depth):
        j = i // N_MIXERS
        if i % N_MIXERS == 0:
            lambda_init = 0.8 - 0.6 * math.exp(-0.3 * i)
            x2 = x.reshape(b * s, d)
            qkv = _norm_matmul(x2, attn_norm[j].reshape(1, d).astype(F32), w_qkv[j].astype(BF16))
            lam_params = jnp.stack([lambda_q1[j], lambda_k1[j], lambda_q2[j], lambda_k2[j]]).astype(F32)
            o = _attention(qkv.reshape(b, s, -1), bias_tiles,
                           jnp.tile(q_gain[j].astype(F32), 2).reshape(HEAD_W, 1),
                           jnp.tile(k_gain[j].astype(F32), 2).reshape(HEAD_W, 1),
                           lam_params, subln_gain[j].reshape(HEAD_W, 1).astype(F32), lambda_init)
            x2 = _matmul_residual(o.reshape(b * s, -1), w_o[j].astype(BF16), x2)
        else:
            x3 = _pool_mixer(x, pool_norm[j].reshape(1, d).astype(F32), pool_w_in[j].astype(BF16),
                             pool_w_group[j].astype(BF16), pool_w_out[j].astype(BF16),
                             pool_scale[j].reshape(1, d).astype(F32))
            x2 = x3.reshape(b * s, d)
        x2 = _hier_moe(x2, ffn_norm[i].reshape(1, d).astype(F32), router_group_w[i], router_group_b[i],
                       router_expert_w[i], router_expert_b[i], w_gate, w_up, w_down, i)
        x = x2.reshape(b, s, d)
    return x
```

```python
import functools
import math

import jax
import jax.numpy as jnp
from jax import lax
from jax.experimental import pallas as pl
from jax.experimental.pallas import tpu as pltpu

F32 = jnp.float32
BF16 = jnp.bfloat16
U32 = jnp.uint32
I32 = jnp.int32

EPS = 1e-6
HEAD_DIM = 64
HEAD_W = 2 * HEAD_DIM
NUM_BUCKETS = 32
MAX_EXACT = NUM_BUCKETS // 2
MAX_DISTANCE = 128
POOL_WINDOWS = (2, 4, 8, 16)
N_EXPERT_GROUPS = 4
EXPERTS_PER_GROUP = 8
N_EXPERTS = N_EXPERT_GROUPS * EXPERTS_PER_GROUP
TOP_K = 2
N_MIXERS = 2
LOG2E = math.log2(math.e)

LANES = 128
VMEM_LIMIT = 48 * 1024 * 1024
ROW_TILE = 512
ATTN_TILE = 512
ATTN_PREP_ROWS = 256
VT_ROWS = HEAD_W + 16
EXPERT_BLOCK = 512
MOVE_TILE = 256
ISSUE_UNROLL = 16
POOL_HALO = 16
MASK_VALUE = -1e30
FAR, SUBDIAG, DIAG = 0, 1, 2
ROUTER_COLS = LANES


def _params(*sem):
    return pltpu.CompilerParams(dimension_semantics=sem, vmem_limit_bytes=VMEM_LIMIT)


def _rms(x, gain):
    ms = jnp.mean(x * x, axis=-1, keepdims=True)
    return x * lax.rsqrt(ms + EPS) * gain


def _pack_rows(v):
    c = v.shape[1] // 2
    bits = lax.bitcast_convert_type(v.astype(BF16).astype(F32), U32)
    return (bits[:, :c] >> 16) | (bits[:, c:] & jnp.uint32(0xFFFF0000))


def _unpack_rows(w):
    lo = lax.bitcast_convert_type(w << 16, F32)
    hi = lax.bitcast_convert_type(w & jnp.uint32(0xFFFF0000), F32)
    return jnp.concatenate([lo, hi], axis=1)


def _norm_matmul_kernel(x_ref, g_ref, w_ref, o_ref, *, n_chunks):
    h = _rms(x_ref[...], g_ref[...]).astype(BF16)
    cw = w_ref.shape[1] // n_chunks
    for c in range(n_chunks):
        r = jnp.dot(h, w_ref[:, c * cw:(c + 1) * cw], preferred_element_type=F32).astype(o_ref.dtype)
        for j in range(cw // HEAD_W):
            o_ref[c * (cw // HEAD_W) + j] = r[:, j * HEAD_W:(j + 1) * HEAD_W]


def _norm_matmul(x2, gain, w):
    t, d = x2.shape
    n = w.shape[1]
    tm = min(ROW_TILE, t)
    return pl.pallas_call(
        functools.partial(_norm_matmul_kernel, n_chunks=n // d),
        grid=(t // tm,),
        in_specs=[pl.BlockSpec((tm, d), lambda i: (i, 0)),
                  pl.BlockSpec((1, d), lambda i: (0, 0)),
                  pl.BlockSpec((d, n), lambda i: (0, 0))],
        out_specs=pl.BlockSpec((n // HEAD_W, tm, HEAD_W), lambda i: (0, i, 0)),
        out_shape=jax.ShapeDtypeStruct((n // HEAD_W, t, HEAD_W), BF16),
        compiler_params=_params("parallel"),
        name="norm_matmul",
    )(x2, gain, w)


def _matmul_residual_kernel(a_ref, w_ref, x_ref, o_ref):
    a = jnp.concatenate([a_ref[h] for h in range(a_ref.shape[0])], axis=1)
    o_ref[...] = x_ref[...] + jnp.dot(a, w_ref[...], preferred_element_type=F32)


def _matmul_residual(a, w, x2):
    t, d = x2.shape
    k = w.shape[0]
    tm = min(ROW_TILE, t)
    return pl.pallas_call(
        _matmul_residual_kernel,
        grid=(t // tm,),
        in_specs=[pl.BlockSpec((k // HEAD_W, tm, HEAD_W), lambda i: (0, i, 0)),
                  pl.BlockSpec((k, d), lambda i: (0, 0)),
                  pl.BlockSpec((tm, d), lambda i: (i, 0))],
        out_specs=pl.BlockSpec((tm, d), lambda i: (i, 0)),
        out_shape=jax.ShapeDtypeStruct((t, d), F32),
        compiler_params=_params("parallel"),
        name="matmul_residual",
    )(a, w, x2)


def _rel_bucket(dist):
    n = jnp.maximum(dist, 0)
    nf = jnp.maximum(n, 1).astype(F32)
    large = MAX_EXACT + (jnp.log(nf / MAX_EXACT) / math.log(MAX_DISTANCE / MAX_EXACT)
                         * (NUM_BUCKETS - MAX_EXACT)).astype(I32)
    large = jnp.minimum(large, NUM_BUCKETS - 1)
    return jnp.where(n < MAX_EXACT, n, large)


def _bias_tiles(rel_bias):
    assert ATTN_TILE >= MAX_DISTANCE
    kj = jnp.arange(ATTN_TILE)[:, None]
    qi = jnp.arange(ATTN_TILE)[None, :]
    table = rel_bias.astype(F32)
    vals = (table - table[NUM_BUCKETS - 1]) * LOG2E
    tiles = []
    for off in (0, ATTN_TILE):
        dist = qi - kj + off
        bucket = _rel_bucket(dist)[None]
        b = jnp.zeros((table.shape[1], ATTN_TILE, ATTN_TILE), F32)
        for n in range(NUM_BUCKETS):
            b = jnp.where(bucket == n, vals[n][:, None, None], b)
        tiles.append(jnp.where((dist >= 0)[None], b, MASK_VALUE))
    return jnp.stack(tiles, axis=1)


def _attn_schedule(n_q_tiles):
    far = [(kj, qi, 0) for qi in range(n_q_tiles) for kj in range(qi - 1)]
    near = []
    for qi in range(n_q_tiles):
        if qi >= 1:
            near.append((qi - 1, qi, 1))
        near.append((qi, qi, 0))
    return far, near


def _attn_kernel(sk_ref, sq_ref, q_ref, k_ref, v_ref, bias_ref, qgc_ref, kgc_ref, lam_ref, sgc_ref,
                 o_ref, kn_ref, vt_ref, qpt1_ref, qpt2_ref, sa1_ref, sa2_ref, sb1_ref, sb2_ref,
                 acc1_ref, acc2_ref, m1_ref, m2_ref, *, lambda_init, n_far, n_steps):
    qpt_refs, acc_refs, m_refs = (qpt1_ref, qpt2_ref), (acc1_ref, acc2_ref), (m1_ref, m2_ref)
    buf_a, buf_b = (sa1_ref, sa2_ref), (sb1_ref, sb2_ref)
    seq = k_ref.shape[1]
    tq = ATTN_TILE
    pr = ATTN_PREP_ROWS
    nt = (((1,), (1,)), ((), ()))

    def identity(n):
        return jnp.where(lax.broadcasted_iota(I32, (n, n), 0) == lax.broadcasted_iota(I32, (n, n), 1),
                         1.0, 0.0).astype(BF16)

    eye_w, eye_rows = identity(HEAD_W), identity(pr)
    first_rows = lax.broadcasted_iota(I32, (HEAD_W, 1), 0) < HEAD_DIM

    def transposed(eye, x):
        return lax.dot_general(eye, x, nt, preferred_element_type=F32)

    def half_norm_t(xt, gain_col):
        sq = xt * xt
        r1 = lax.rsqrt(jnp.sum(sq[:HEAD_DIM], axis=0, keepdims=True) / HEAD_DIM + EPS)
        r2 = lax.rsqrt(jnp.sum(sq[HEAD_DIM:], axis=0, keepdims=True) / HEAD_DIM + EPS)
        return xt * jnp.where(first_rows, r1, r2) * gain_col

    def prep(c, carry):
        r0 = pl.multiple_of(c * pr, pr)
        rows = pl.ds(r0, pr)
        vt_ref[0:HEAD_W, rows] = transposed(eye_w, v_ref[0, rows, :]).astype(BF16)
        knt = half_norm_t(transposed(eye_w, k_ref[0, rows, :]), kgc_ref[...]).astype(BF16)
        kn_ref[rows, :] = transposed(eye_rows, knt).astype(BF16)
        qnt = half_norm_t(transposed(eye_w, q_ref[0, rows, :]), qgc_ref[...]) * (HEAD_DIM ** -0.5 * LOG2E)
        qpt1_ref[:, rows] = jnp.where(first_rows, qnt, 0.0).astype(BF16)
        qpt2_ref[:, rows] = jnp.where(first_rows, 0.0, qnt).astype(BF16)
        return carry

    lax.fori_loop(0, seq // pr, prep, 0, unroll=4)
    ones_row = lax.broadcasted_iota(I32, (VT_ROWS - HEAD_W, seq), 0) == 0
    vt_ref[HEAD_W:VT_ROWS, :] = jnp.where(ones_row, 1.0, 0.0).astype(BF16)
    for c in range(2):
        m_refs[c][...] = jnp.full(m_refs[c].shape, MASK_VALUE, F32)
        acc_refs[c][...] = jnp.zeros(acc_refs[c].shape, F32)

    def tiles(t):
        return (pl.ds(pl.multiple_of(sk_ref[t] * tq, tq), tq), pl.ds(pl.multiple_of(sq_ref[t] * tq, tq), tq))

    def logits(t, dst):
        keys, queries = tiles(t)
        kt = kn_ref[keys, :]
        for c in range(2):
            dst[c][...] = jnp.dot(kt, qpt_refs[c][:, queries], preferred_element_type=F32)

    def kind(t):
        if t < n_far:
            return FAR
        return DIAG if (t - n_far) % 2 == 0 else SUBDIAG

    corner = (slice(tq - MAX_DISTANCE, tq), slice(0, MAX_DISTANCE))

    def absorb(t, src, step_kind):
        keys, queries = tiles(t)
        vt = vt_ref[:, keys]
        for c in range(2):
            if step_kind == SUBDIAG:
                src[c][corner] = src[c][corner] + bias_ref[(0, 1) + corner]
            s = src[c][...]
            if step_kind == DIAG:
                s = s + bias_ref[0, 0]
            m_old = m_refs[c][:, queries]
            m_new = jnp.maximum(m_old, jnp.max(s, axis=0, keepdims=True))
            m_refs[c][:, queries] = m_new
            p = jnp.exp2(s - m_new).astype(BF16)
            pv = jnp.dot(vt, p, preferred_element_type=F32)
            acc_refs[c][:, queries] = jnp.exp2(m_old - m_new) * acc_refs[c][:, queries] + pv

    def pair(t0, kinds, trailing):
        logits(t0 + 1, buf_b)
        absorb(t0, buf_a, kinds[0])
        if trailing:
            logits(t0 + 2, buf_a)
        absorb(t0 + 1, buf_b, kinds[1])

    def rolled(first, count):
        kinds = (kind(first), kind(first + 1))

        def body(i, carry):
            pair(first + 2 * i, kinds, True)
            return carry
        if count > 0:
            lax.fori_loop(0, count, body, 0)

    logits(0, buf_a)
    rolled(0, n_far // 2)
    t = 2 * (n_far // 2)
    if n_far % 2:
        pair(t, (kind(t), kind(t + 1)), t + 2 < n_steps)
        t += 2
    rest = n_steps - t
    n_rolled = max(rest // 2 - (1 - rest % 2), 0)
    rolled(t, n_rolled)
    t += 2 * n_rolled
    if rest // 2 > n_rolled:
        pair(t, (kind(t), kind(t + 1)), False)
        t += 2
    if rest % 2:
        absorb(t, buf_a, kind(t))

    lp = lam_ref[...]
    lam = (jnp.exp(jnp.sum(lp[0:1] * lp[1:2], axis=-1, keepdims=True))
           - jnp.exp(jnp.sum(lp[2:3] * lp[3:4], axis=-1, keepdims=True)) + lambda_init)
    out_gain = sgc_ref[...] * (1.0 - lambda_init)

    def finish(c, carry):
        queries = pl.ds(pl.multiple_of(c * tq, tq), tq)
        a1 = acc1_ref[:, queries]
        a2 = acc2_ref[:, queries]
        ot = a1[:HEAD_W] / a1[HEAD_W:HEAD_W + 1] - lam * (a2[:HEAD_W] / a2[HEAD_W:HEAD_W + 1])
        ms = jnp.mean(ot * ot, axis=0, keepdims=True)
        ot = ot * lax.rsqrt(ms + EPS) * out_gain
        o_ref[0, queries, :] = ot.T.astype(o_ref.dtype)
        return carry

    lax.fori_loop(0, seq // tq, finish, 0)


def _attention(qkv, bias_tiles, q_gain2, k_gain2, lam_params, subln_gain_col, lambda_init):
    n_slabs, b, s, _ = qkv.shape
    n_heads = n_slabs // 3
    assert s % ATTN_TILE == 0
    tq = ATTN_TILE
    far, near = _attn_schedule(s // tq)
    steps = jnp.asarray(far + near, I32)

    def im(f):
        return lambda bi, h, sk, sq: f(bi, h)

    grid_spec = pltpu.PrefetchScalarGridSpec(
        num_scalar_prefetch=2,
        grid=(b, n_heads),
        in_specs=[pl.BlockSpec((None, 1, s, HEAD_W), im(lambda bi, h: (h, bi, 0, 0))),
                  pl.BlockSpec((None, 1, s, HEAD_W), im(lambda bi, h: (n_heads + h, bi, 0, 0))),
                  pl.BlockSpec((None, 1, s, HEAD_W), im(lambda bi, h: (2 * n_heads + h, bi, 0, 0))),
                  pl.BlockSpec((1, 2, tq, tq), im(lambda bi, h: (h, 0, 0, 0))),
                  pl.BlockSpec((HEAD_W, 1), im(lambda bi, h: (0, 0))),
                  pl.BlockSpec((HEAD_W, 1), im(lambda bi, h: (0, 0))),
                  pl.BlockSpec((4, HEAD_DIM), im(lambda bi, h: (0, 0))),
                  pl.BlockSpec((HEAD_W, 1), im(lambda bi, h: (0, 0)))],
        out_specs=pl.BlockSpec((None, 1, s, HEAD_W), im(lambda bi, h: (h, bi, 0, 0))),
        scratch_shapes=[pltpu.VMEM((s, HEAD_W), BF16),
                        pltpu.VMEM((VT_ROWS, s), BF16)]
        + [pltpu.VMEM((HEAD_W, s), BF16)] * 2
        + [pltpu.VMEM((tq, tq), F32)] * 4
        + [pltpu.VMEM((VT_ROWS, s), F32)] * 2
        + [pltpu.VMEM((1, s), F32)] * 2,
    )
    return pl.pallas_call(
        functools.partial(_attn_kernel, lambda_init=lambda_init, n_far=len(far), n_steps=len(far) + len(near)),
        grid_spec=grid_spec,
        out_shape=jax.ShapeDtypeStruct((n_heads, b, s, HEAD_W), BF16),
        compiler_params=_params("parallel", "parallel"),
        name="diff_attention",
    )(steps[:, 0], steps[:, 1], qkv, qkv, qkv, bias_tiles, q_gain2, k_gain2, lam_params,
      subln_gain_col)


def _pool_kernel(x_ref, g_ref, win_ref, wgrp_ref, wout_ref, scale_ref, o_ref, ext_ref, pooled_ref):
    j = pl.program_id(1)
    tm = x_ref.shape[1]
    d = x_ref.shape[2]
    gw = d // len(POOL_WINDOWS)
    x = x_ref[0]

    @pl.when(j == 0)
    def _():
        ext_ref[0:POOL_HALO, :] = jnp.zeros((POOL_HALO, d), F32)

    @pl.when(j > 0)
    def _():
        ext_ref[0:POOL_HALO, :] = ext_ref[tm:tm + POOL_HALO, :]

    h = _rms(x, g_ref[...]).astype(BF16)
    ext_ref[POOL_HALO:POOL_HALO + tm, :] = jnp.dot(h, win_ref[...], preferred_element_type=F32)

    pos1 = (j * tm + 1 + lax.broadcasted_iota(I32, (tm, 1), 0)).astype(F32)
    for g, win in enumerate(POOL_WINDOWS):
        c0, c1 = g * gw, (g + 1) * gw
        u = ext_ref[POOL_HALO:POOL_HALO + tm, c0:c1]
        s = u
        for k in range(1, win):
            s = s + ext_ref[POOL_HALO - k:POOL_HALO - k + tm, c0:c1]
        inv_cnt = 1.0 / jnp.minimum(pos1, float(win))
        pooled = (s * inv_cnt - u).astype(BF16)
        pooled_ref[:, c0:c1] = jnp.dot(pooled, wgrp_ref[g], preferred_element_type=F32).astype(BF16)

    y = jnp.dot(pooled_ref[...], wout_ref[...], preferred_element_type=F32)
    o_ref[0] = x + y * scale_ref[...]


def _pool_mixer(x, gain, w_in, w_group, w_out, scale):
    b, s, d = x.shape
    tm = min(ROW_TILE, s)
    ng = len(POOL_WINDOWS)
    gw = d // ng
    return pl.pallas_call(
        _pool_kernel,
        grid=(b, s // tm),
        in_specs=[pl.BlockSpec((1, tm, d), lambda bi, j: (bi, j, 0)),
                  pl.BlockSpec((1, d), lambda bi, j: (0, 0)),
                  pl.BlockSpec((d, d), lambda bi, j: (0, 0)),
                  pl.BlockSpec((ng, gw, gw), lambda bi, j: (0, 0, 0)),
                  pl.BlockSpec((d, d), lambda bi, j: (0, 0)),
                  pl.BlockSpec((1, d), lambda bi, j: (0, 0))],
        out_specs=pl.BlockSpec((1, tm, d), lambda bi, j: (bi, j, 0)),
        out_shape=jax.ShapeDtypeStruct((b, s, d), F32),
        scratch_shapes=[pltpu.VMEM((POOL_HALO + tm, d), F32),
                        pltpu.VMEM((tm, d), BF16)],
        compiler_params=_params("arbitrary", "arbitrary"),
        name="pool_mixer",
    )(x, gain, w_in, w_group, w_out, scale)


def _router_kernel(x_ref, g_ref, wr_ref, br_ref, o_ref, cnt_ref, carry_ref):
    tm = x_ref.shape[0]

    @pl.when(pl.program_id(0) == 0)
    def _():
        carry_ref[...] = jnp.zeros(carry_ref.shape, F32)

    h = _rms(x_ref[...], g_ref[...])
    h_hi = h.astype(BF16)
    h_lo = (h - h_hi.astype(F32)).astype(BF16)
    w = wr_ref[...]
    half = ROUTER_COLS // 2
    nt = (((1,), (1,)), ((), ()))
    a = lax.dot_general(w, h_hi, nt, preferred_element_type=F32)
    b = lax.dot_general(w, h_lo, nt, preferred_element_type=F32)
    lt = a[:half] + a[half:] + b[:half] + br_ref[...]

    def row(r):
        return lt[r:r + 1, :]

    g_best = row(N_EXPERTS)
    g_idx = jnp.zeros((1, tm), I32)
    for g in range(1, N_EXPERT_GROUPS):
        v = row(N_EXPERTS + g)
        better = v > g_best
        g_best = jnp.where(better, v, g_best)
        g_idx = jnp.where(better, g, g_idx)
    denom = jnp.zeros((1, tm), F32)
    for g in range(N_EXPERT_GROUPS):
        denom = denom + jnp.exp(row(N_EXPERTS + g) - g_best)
    g_prob = 1.0 / denom

    sel = []
    for e in range(EXPERTS_PER_GROUP):
        v = row(e)
        for g in range(1, N_EXPERT_GROUPS):
            v = jnp.where(g_idx == g, row(g * EXPERTS_PER_GROUP + e), v)
        sel.append(v)

    def top1(vals):
        best, idx = vals[0], jnp.zeros((1, tm), I32)
        for e in range(1, EXPERTS_PER_GROUP):
            better = vals[e] > best
            best = jnp.where(better, vals[e], best)
            idx = jnp.where(better, e, idx)
        return best, idx

    v0, i0 = top1(sel)
    v1, i1 = top1([jnp.where(i0 == e, -jnp.inf, sel[e]) for e in range(EXPERTS_PER_GROUP)])
    t = jnp.exp(v1 - v0)
    w0 = g_prob / (1.0 + t)
    w1 = w0 * t
    e0 = g_idx * EXPERTS_PER_GROUP + i0
    e1 = g_idx * EXPERTS_PER_GROUP + i1

    eid = lax.broadcasted_iota(I32, (N_EXPERTS, tm), 0)
    oh0 = eid == e0
    oh1 = eid == e1
    onehot = jnp.where(oh0 | oh1, 1.0, 0.0)
    earlier = (lax.broadcasted_iota(I32, (tm, tm), 0) < lax.broadcasted_iota(I32, (tm, tm), 1))
    prefix = jnp.dot(onehot.astype(BF16), jnp.where(earlier, 1.0, 0.0).astype(BF16),
                     preferred_element_type=F32)
    before = prefix + carry_ref[:, 0:1]
    rank0 = jnp.sum(jnp.where(oh0, before, 0.0), axis=0, keepdims=True)
    rank1 = jnp.sum(jnp.where(oh1, before, 0.0), axis=0, keepdims=True)
    carry_ref[...] = carry_ref[...] + jnp.sum(onehot, axis=1, keepdims=True)
    cnt_ref[...] = carry_ref[...]

    o_ref[...] = jnp.concatenate(
        [e0.astype(F32), e1.astype(F32), w0, w1, rank0, rank1, jnp.zeros((2, tm), F32)], axis=0)


def _router(x2, gain, wr, br):
    t, d = x2.shape
    tm = min(ROW_TILE, t)
    assert t * TOP_K < 2 ** 24
    return pl.pallas_call(
        _router_kernel,
        grid=(t // tm,),
        in_specs=[pl.BlockSpec((tm, d), lambda i: (i, 0)),
                  pl.BlockSpec((1, d), lambda i: (0, 0)),
                  pl.BlockSpec((ROUTER_COLS, d), lambda i: (0, 0)),
                  pl.BlockSpec((ROUTER_COLS // 2, 1), lambda i: (0, 0))],
        out_specs=[pl.BlockSpec((8, tm), lambda i: (0, i)),
                   pl.BlockSpec((N_EXPERTS, LANES), lambda i: (0, 0))],
        out_shape=[jax.ShapeDtypeStruct((8, t), F32),
                   jax.ShapeDtypeStruct((N_EXPERTS, LANES), F32)],
        scratch_shapes=[pltpu.VMEM((N_EXPERTS, LANES), F32)],
        compiler_params=_params("arbitrary"),
        name="router",
    )(x2, gain, wr, br)


def _slot_positions(route, counts_f, blk, n_blocks):
    e = route[0:TOP_K].astype(I32).T.reshape(-1)
    rank = route[2 * TOP_K:3 * TOP_K].astype(I32).T.reshape(-1)
    counts = counts_f[:, 0].astype(I32)
    padded = (counts + blk - 1) // blk * blk
    pad_end = jnp.cumsum(padded)
    pad_start = pad_end - padded
    experts = jnp.arange(N_EXPERTS, dtype=I32)
    pos = rank + jnp.sum(jnp.where(e[:, None] == experts[None, :], pad_start[None, :], 0), axis=1)
    first_row = jnp.arange(n_blocks, dtype=I32) * blk
    block_expert = jnp.minimum(
        jnp.sum((pad_end[None, :] <= first_row[:, None]).astype(I32), axis=1), N_EXPERTS - 1)
    n_used = (pad_end[-1] // blk).reshape(1)
    return pos.astype(I32), block_expert.astype(I32), n_used.astype(I32)


def _row_copy(src_ref, src_row, dst_ref, dst_row, sem):
    return pltpu.make_async_copy(src_ref.at[src_row], dst_ref.at[dst_row], sem)


def _dispatch_kernel(pos_ref, x_ref, g_ref, xs_in_ref, xs_ref, buf_ref, sem):
    del xs_in_ref
    i = pl.program_id(0)
    n = pl.num_programs(0)
    tm = x_ref.shape[0]
    slot = i % 2

    def wait_slot(sl):
        for _ in range(TOP_K):
            pltpu.make_async_copy(buf_ref.at[sl], xs_ref.at[pl.ds(0, tm)], sem.at[sl]).wait()

    @pl.when(i >= 2)
    def _():
        wait_slot(slot)

    packed = _pack_rows(_rms(x_ref[...], g_ref[...]))
    buf_ref[slot] = packed.reshape(tm, packed.shape[1] // LANES, LANES)

    def issue(g, c):
        base = g * ISSUE_UNROLL
        for u in range(ISSUE_UNROLL):
            for k in range(TOP_K):
                p = pos_ref[(i * tm + base + u) * TOP_K + k]
                _row_copy(buf_ref.at[slot], base + u, xs_ref, p, sem.at[slot]).start()
        return c

    lax.fori_loop(0, tm // ISSUE_UNROLL, issue, 0)

    @pl.when(i == n - 1)
    def _():
        wait_slot(slot)

    @pl.when((i == n - 1) & (n >= 2))
    def _():
        wait_slot(1 - slot)


def _dispatch(pos, x2, gain, n_slots):
    t, d = x2.shape
    tm = min(MOVE_TILE, t)
    chunks = d // 2 // LANES
    zeros = jnp.zeros((n_slots, chunks, LANES), U32)
    grid_spec = pltpu.PrefetchScalarGridSpec(
        num_scalar_prefetch=1,
        grid=(t // tm,),
        in_specs=[pl.BlockSpec((tm, d), lambda i, pos: (i, 0)),
                  pl.BlockSpec((1, d), lambda i, pos: (0, 0)),
                  pl.BlockSpec(memory_space=pl.ANY)],
        out_specs=pl.BlockSpec(memory_space=pl.ANY),
        scratch_shapes=[pltpu.VMEM((2, tm, chunks, LANES), U32),
                        pltpu.SemaphoreType.DMA((2,))],
    )
    slabs = pl.pallas_call(
        _dispatch_kernel,
        grid_spec=grid_spec,
        out_shape=jax.ShapeDtypeStruct((n_slots, chunks, LANES), U32),
        input_output_aliases={3: 0},
        compiler_params=_params("arbitrary"),
        name="dispatch",
    )(pos, x2, gain, zeros)
    return slabs


def _expert_kernel(be_ref, nu_ref, xs_ref, wg_ref, wu_ref, wd_ref, y_ref, wgb_ref, wub_ref, wdb_ref):
    i = pl.program_id(0)
    e = be_ref[i]
    prev = be_ref[jnp.maximum(i - 1, 0)]

    @pl.when((i == 0) | (e != prev))
    def _():
        wgb_ref[...] = wg_ref[0, 0].astype(BF16)
        wub_ref[...] = wu_ref[0, 0].astype(BF16)
        wdb_ref[...] = wd_ref[0, 0].astype(BF16)

    @pl.when(i < nu_ref[0])
    def _():
        blk, chunks, lanes = xs_ref.shape
        x = _unpack_rows(xs_ref[...].reshape(blk, chunks * lanes)).astype(BF16)
        g = jnp.dot(x, wgb_ref[...], preferred_element_type=F32)
        u = jnp.dot(x, wub_ref[...], preferred_element_type=F32)
        a = (g / (1.0 + jnp.exp(-g)) * u).astype(BF16)
        y = _pack_rows(jnp.dot(a, wdb_ref[...], preferred_element_type=F32))
        y_ref[...] = y.reshape(blk, chunks, lanes)

    @pl.when(i >= nu_ref[0])
    def _():
        y_ref[...] = jnp.zeros(y_ref.shape, U32)


def _experts(block_expert, n_used, xs, w_gate, w_up, w_down, layer):
    n_slots, chunks, lanes = xs.shape
    blk = EXPERT_BLOCK
    d, ff = w_gate.shape[2], w_gate.shape[3]
    grid_spec = pltpu.PrefetchScalarGridSpec(
        num_scalar_prefetch=2,
        grid=(n_slots // blk,),
        in_specs=[pl.BlockSpec((blk, chunks, lanes), lambda i, be, nu: (i, 0, 0)),
                  pl.BlockSpec((1, 1, d, ff), lambda i, be, nu: (layer, be[i], 0, 0)),
                  pl.BlockSpec((1, 1, d, ff), lambda i, be, nu: (layer, be[i], 0, 0)),
                  pl.BlockSpec((1, 1, ff, d), lambda i, be, nu: (layer, be[i], 0, 0))],
        out_specs=pl.BlockSpec((blk, chunks, lanes), lambda i, be, nu: (i, 0, 0)),
        scratch_shapes=[pltpu.VMEM((d, ff), BF16),
                        pltpu.VMEM((d, ff), BF16),
                        pltpu.VMEM((ff, d), BF16)],
    )
    return pl.pallas_call(
        _expert_kernel,
        grid_spec=grid_spec,
        out_shape=jax.ShapeDtypeStruct((n_slots, chunks, lanes), U32),
        compiler_params=_params("arbitrary"),
        name="experts",
    )(block_expert, n_used, xs, w_gate, w_up, w_down)


def _combine_kernel(pos_ref, y_ref, x_ref, w_ref, o_ref, buf_ref, sem):
    i = pl.program_id(0)
    n = pl.num_programs(0)
    tm = x_ref.shape[0]
    slot = i % 2

    def issue_tile(tile, sl):
        def body(g, c):
            base = pl.multiple_of(g * ISSUE_UNROLL, ISSUE_UNROLL)
            for u in range(ISSUE_UNROLL):
                for k in range(TOP_K):
                    p = pos_ref[(tile * tm + base + u) * TOP_K + k]
                    _row_copy(y_ref, p, buf_ref.at[sl], k * tm + base + u, sem.at[sl]).start()
            return c
        lax.fori_loop(0, tm // ISSUE_UNROLL, body, 0)

    @pl.when(i == 0)
    def _():
        issue_tile(0, 0)

    @pl.when(i + 1 < n)
    def _():
        issue_tile(i + 1, 1 - slot)

    pltpu.make_async_copy(y_ref.at[pl.ds(0, TOP_K * tm)], buf_ref.at[slot], sem.at[slot]).wait()

    slabs = buf_ref[slot]
    rows = slabs.reshape(slabs.shape[0], slabs.shape[1] * slabs.shape[2])
    w = w_ref[...]
    out = x_ref[...]
    for k in range(TOP_K):
        out = out + w[:, k:k + 1] * _unpack_rows(rows[k * tm:(k + 1) * tm])
    o_ref[...] = out


def _combine(pos, y, x2, weights):
    t, d = x2.shape
    tm = min(MOVE_TILE, t)
    chunks = d // 2 // LANES
    grid_spec = pltpu.PrefetchScalarGridSpec(
        num_scalar_prefetch=1,
        grid=(t // tm,),
        in_specs=[pl.BlockSpec(memory_space=pl.ANY),
                  pl.BlockSpec((tm, d), lambda i, pos: (i, 0)),
                  pl.BlockSpec((tm, TOP_K), lambda i, pos: (i, 0))],
        out_specs=pl.BlockSpec((tm, d), lambda i, pos: (i, 0)),
        scratch_shapes=[pltpu.VMEM((2, TOP_K * tm, chunks, LANES), U32),
                        pltpu.SemaphoreType.DMA((2,))],
    )
    return pl.pallas_call(
        _combine_kernel,
        grid_spec=grid_spec,
        out_shape=jax.ShapeDtypeStruct((t, d), F32),
        compiler_params=_params("arbitrary"),
        name="combine",
    )(pos, y, x2, weights)


def _hier_moe(x2, gain, wg1, bg1, wg2, bg2, w_gate, w_up, w_down, layer):
    t, d = x2.shape
    half = ROUTER_COLS // 2
    pad = half - N_EXPERTS - N_EXPERT_GROUPS
    w_f32 = jnp.concatenate([wg2.astype(F32), wg1.astype(F32), jnp.zeros((d, pad), F32)], axis=1)
    w_hi = w_f32.astype(BF16)
    w_lo = (w_f32 - w_hi.astype(F32)).astype(BF16)
    wr = jnp.concatenate([w_hi, w_lo], axis=1).T
    br = jnp.concatenate([bg2.astype(F32), bg1.astype(F32), jnp.zeros((pad,), F32)]).reshape(half, 1)
    route, counts = _router(x2, gain, wr, br)

    blk = EXPERT_BLOCK
    n_blocks = -(-(t * TOP_K) // blk) + N_EXPERTS
    pos, block_expert, n_used = _slot_positions(route, counts, blk, n_blocks)
    xs = _dispatch(pos, x2, gain, n_blocks * blk)
    y = _experts(block_expert, n_used, xs, w_gate, w_up, w_down, layer)
    return _combine(pos, y, x2, route[TOP_K:2 * TOP_K].T)


def kernel(x, rel_bias, attn_norm, w_qkv, q_gain, k_gain, lambda_q1, lambda_k1, lambda_q2, lambda_k2,
           subln_gain, w_o, pool_norm, pool_w_in, pool_w_group, pool_w_out, pool_scale,
           ffn_norm, router_group_w, router_group_b, router_expert_w, router_expert_b,
           w_gate, w_up, w_down):
    b, s, d = x.shape
    depth = ffn_norm.shape[0]
    x = x.astype(F32)
    bias_tiles = _bias_tiles(rel_bias)
    for i in range(depth):
        j = i // N_MIXERS
        if i % N_MIXERS == 0:
            lambda_init = 0.8 - 0.6 * math.exp(-0.3 * i)
            x2 = x.reshape(b * s, d)
            qkv = _norm_matmul(x2, attn_norm[j].reshape(1, d).astype(F32), w_qkv[j].astype(BF16))
            lam_params = jnp.stack([lambda_q1[j], lambda_k1[j], lambda_q2[j], lambda_k2[j]]).astype(F32)
            o = _attention(qkv.reshape(-1, b, s, HEAD_W), bias_tiles,
                           jnp.tile(q_gain[j].astype(F32), 2).reshape(HEAD_W, 1),
                           jnp.tile(k_gain[j].astype(F32), 2).reshape(HEAD_W, 1),
                           lam_params, subln_gain[j].reshape(HEAD_W, 1).astype(F32), lambda_init)
            x2 = _matmul_residual(o.reshape(-1, b * s, HEAD_W), w_o[j].astype(BF16), x2)
        else:
            x3 = _pool_mixer(x, pool_norm[j].reshape(1, d).astype(F32), pool_w_in[j].astype(BF16),
                             pool_w_group[j].astype(BF16), pool_w_out[j].astype(BF16),
                             pool_scale[j].reshape(1, d).astype(F32))
            x2 = x3.reshape(b * s, d)
        x2 = _hier_moe(x2, ffn_norm[i].reshape(1, d).astype(F32), router_group_w[i], router_group_b[i],
                       router_expert_w[i], router_expert_b[i], w_gate, w_up, w_down, i)
        x = x2.reshape(b, s, d)
    return x
```

```python
import functools
import math

import jax
import jax.numpy as jnp
from jax import lax
from jax.experimental import pallas as pl
from jax.experimental.pallas import tpu as pltpu

F32 = jnp.float32
BF16 = jnp.bfloat16
U32 = jnp.uint32
I32 = jnp.int32

EPS = 1e-6
HEAD_DIM = 64
HEAD_W = 2 * HEAD_DIM
NUM_BUCKETS = 32
MAX_EXACT = NUM_BUCKETS // 2
MAX_DISTANCE = 128
POOL_WINDOWS = (2, 4, 8, 16)
N_EXPERT_GROUPS = 4
EXPERTS_PER_GROUP = 8
N_EXPERTS = N_EXPERT_GROUPS * EXPERTS_PER_GROUP
TOP_K = 2
N_MIXERS = 2
LOG2E = math.log2(math.e)

LANES = 128
VMEM_LIMIT = 48 * 1024 * 1024
ROW_TILE = 512
ATTN_TILE = 512
ATTN_PREP_ROWS = 256
VT_ROWS = HEAD_W + 16
EXPERT_BLOCK = 512
MOVE_TILE = 256
ISSUE_UNROLL = 16
POOL_HALO = 16
MASK_VALUE = -1e30
FAR, SUBDIAG, DIAG = 0, 1, 2
ROUTER_COLS = LANES


def _params(*sem):
    return pltpu.CompilerParams(dimension_semantics=sem, vmem_limit_bytes=VMEM_LIMIT)


def _rms(x, gain):
    ms = jnp.mean(x * x, axis=-1, keepdims=True)
    return x * lax.rsqrt(ms + EPS) * gain


def _pack_rows(v):
    c = v.shape[1] // 2
    bits = lax.bitcast_convert_type(v.astype(BF16).astype(F32), U32)
    return (bits[:, :c] >> 16) | (bits[:, c:] & jnp.uint32(0xFFFF0000))


def _unpack_rows(w):
    lo = lax.bitcast_convert_type(w << 16, F32)
    hi = lax.bitcast_convert_type(w & jnp.uint32(0xFFFF0000), F32)
    return jnp.concatenate([lo, hi], axis=1)


def _norm_matmul_kernel(x_ref, g_ref, w_ref, o_ref, *, n_chunks):
    h = _rms(x_ref[...], g_ref[...]).astype(BF16)
    cw = w_ref.shape[1] // n_chunks
    for c in range(n_chunks):
        r = jnp.dot(h, w_ref[:, c * cw:(c + 1) * cw], preferred_element_type=F32).astype(o_ref.dtype)
        for j in range(cw // HEAD_W):
            o_ref[c * (cw // HEAD_W) + j] = r[:, j * HEAD_W:(j + 1) * HEAD_W]


def _norm_matmul(x2, gain, w):
    t, d = x2.shape
    n = w.shape[1]
    tm = min(ROW_TILE, t)
    return pl.pallas_call(
        functools.partial(_norm_matmul_kernel, n_chunks=n // d),
        grid=(t // tm,),
        in_specs=[pl.BlockSpec((tm, d), lambda i: (i, 0)),
                  pl.BlockSpec((1, d), lambda i: (0, 0)),
                  pl.BlockSpec((d, n), lambda i: (0, 0))],
        out_specs=pl.BlockSpec((n // HEAD_W, tm, HEAD_W), lambda i: (0, i, 0)),
        out_shape=jax.ShapeDtypeStruct((n // HEAD_W, t, HEAD_W), BF16),
        compiler_params=_params("parallel"),
        name="norm_matmul",
    )(x2, gain, w)


def _matmul_residual_kernel(a_ref, w_ref, x_ref, o_ref):
    a = jnp.concatenate([a_ref[h] for h in range(a_ref.shape[0])], axis=1)
    o_ref[...] = x_ref[...] + jnp.dot(a, w_ref[...], preferred_element_type=F32)


def _matmul_residual(a, w, x2):
    t, d = x2.shape
    k = w.shape[0]
    tm = min(ROW_TILE, t)
    return pl.pallas_call(
        _matmul_residual_kernel,
        grid=(t // tm,),
        in_specs=[pl.BlockSpec((k // HEAD_W, tm, HEAD_W), lambda i: (0, i, 0)),
                  pl.BlockSpec((k, d), lambda i: (0, 0)),
                  pl.BlockSpec((tm, d), lambda i: (i, 0))],
        out_specs=pl.BlockSpec((tm, d), lambda i: (i, 0)),
        out_shape=jax.ShapeDtypeStruct((t, d), F32),
        compiler_params=_params("parallel"),
        name="matmul_residual",
    )(a, w, x2)


def _rel_bucket(dist):
    n = jnp.maximum(dist, 0)
    nf = jnp.maximum(n, 1).astype(F32)
    large = MAX_EXACT + (jnp.log(nf / MAX_EXACT) / math.log(MAX_DISTANCE / MAX_EXACT)
                         * (NUM_BUCKETS - MAX_EXACT)).astype(I32)
    large = jnp.minimum(large, NUM_BUCKETS - 1)
    return jnp.where(n < MAX_EXACT, n, large)


def _bias_tiles(rel_bias):
    assert ATTN_TILE >= MAX_DISTANCE
    kj = jnp.arange(ATTN_TILE)[:, None]
    qi = jnp.arange(ATTN_TILE)[None, :]
    table = rel_bias.astype(F32)
    vals = (table - table[NUM_BUCKETS - 1]) * LOG2E
    tiles = []
    for off in (0, ATTN_TILE):
        dist = qi - kj + off
        bucket = _rel_bucket(dist)[None]
        b = jnp.zeros((table.shape[1], ATTN_TILE, ATTN_TILE), F32)
        for n in range(NUM_BUCKETS):
            b = jnp.where(bucket == n, vals[n][:, None, None], b)
        tiles.append(jnp.where((dist >= 0)[None], b, MASK_VALUE))
    return jnp.stack(tiles, axis=1)


def _attn_schedule(n_q_tiles):
    far = [(kj, qi, 0) for qi in range(n_q_tiles) for kj in range(qi - 1)]
    near = []
    for qi in range(n_q_tiles):
        if qi >= 1:
            near.append((qi - 1, qi, 1))
        near.append((qi, qi, 0))
    return far, near


def _attn_kernel(sk_ref, sq_ref, q_ref, k_ref, v_ref, bias_ref, qgc_ref, kgc_ref, lam_ref, sgc_ref,
                 o_ref, kn_ref, vt_ref, qpt1_ref, qpt2_ref, sa1_ref, sa2_ref, sb1_ref, sb2_ref,
                 acc1_ref, acc2_ref, m1_ref, m2_ref, *, lambda_init, n_far, n_steps):
    qpt_refs, acc_refs, m_refs = (qpt1_ref, qpt2_ref), (acc1_ref, acc2_ref), (m1_ref, m2_ref)
    buf_a, buf_b = (sa1_ref, sa2_ref), (sb1_ref, sb2_ref)
    seq = k_ref.shape[1]
    tq = ATTN_TILE
    pr = ATTN_PREP_ROWS
    nt = (((1,), (1,)), ((), ()))

    def identity(n):
        return jnp.where(lax.broadcasted_iota(I32, (n, n), 0) == lax.broadcasted_iota(I32, (n, n), 1),
                         1.0, 0.0).astype(BF16)

    eye_w, eye_rows = identity(HEAD_W), identity(pr)
    first_rows = lax.broadcasted_iota(I32, (HEAD_W, 1), 0) < HEAD_DIM

    def transposed(eye, x):
        return lax.dot_general(eye, x, nt, preferred_element_type=F32)

    def half_norm_t(xt, gain_col):
        sq = xt * xt
        r1 = lax.rsqrt(jnp.sum(sq[:HEAD_DIM], axis=0, keepdims=True) / HEAD_DIM + EPS)
        r2 = lax.rsqrt(jnp.sum(sq[HEAD_DIM:], axis=0, keepdims=True) / HEAD_DIM + EPS)
        return xt * jnp.where(first_rows, r1, r2) * gain_col

    def prep(c, carry):
        r0 = pl.multiple_of(c * pr, pr)
        rows = pl.ds(r0, pr)
        vt_ref[0:HEAD_W, rows] = transposed(eye_w, v_ref[0, rows, :]).astype(BF16)
        knt = half_norm_t(transposed(eye_w, k_ref[0, rows, :]), kgc_ref[...]).astype(BF16)
        kn_ref[rows, :] = transposed(eye_rows, knt).astype(BF16)
        qnt = half_norm_t(transposed(eye_w, q_ref[0, rows, :]), qgc_ref[...]) * (HEAD_DIM ** -0.5 * LOG2E)
        qpt1_ref[:, rows] = jnp.where(first_rows, qnt, 0.0).astype(BF16)
        qpt2_ref[:, rows] = jnp.where(first_rows, 0.0, qnt).astype(BF16)
        return carry

    lax.fori_loop(0, seq // pr, prep, 0, unroll=4)
    ones_row = lax.broadcasted_iota(I32, (VT_ROWS - HEAD_W, seq), 0) == 0
    vt_ref[HEAD_W:VT_ROWS, :] = jnp.where(ones_row, 1.0, 0.0).astype(BF16)
    for c in range(2):
        m_refs[c][...] = jnp.full(m_refs[c].shape, MASK_VALUE, F32)
        acc_refs[c][...] = jnp.zeros(acc_refs[c].shape, F32)

    def kind(t):
        if t >= n_steps:
            return None
        if t < n_far:
            return FAR
        return DIAG if (t - n_far) % 2 == 0 else SUBDIAG

    half = tq // 2

    def parts(step_kind):
        if step_kind == DIAG:
            return (((0, half), (0, half)), ((0, tq), (half, tq)))
        return (((0, tq), (0, tq)),)

    def span(base, lo_hi):
        lo, hi = lo_hi
        return pl.ds(pl.multiple_of(base + lo, half), hi - lo)

    def bases(t):
        return pl.multiple_of(sk_ref[t] * tq, tq), pl.multiple_of(sq_ref[t] * tq, tq)

    def logits(t, dst, step_kind):
        k0, q0 = bases(t)
        for kr, qc in parts(step_kind):
            kt = kn_ref[span(k0, kr), :]
            for c in range(2):
                dst[c][kr[0]:kr[1], qc[0]:qc[1]] = jnp.dot(
                    kt, qpt_refs[c][:, span(q0, qc)], preferred_element_type=F32)

    corner = (slice(tq - MAX_DISTANCE, tq), slice(0, MAX_DISTANCE))

    def absorb(t, src, step_kind):
        k0, q0 = bases(t)
        for kr, qc in parts(step_kind):
            block = (slice(kr[0], kr[1]), slice(qc[0], qc[1]))
            queries = span(q0, qc)
            vt = vt_ref[:, span(k0, kr)]
            for c in range(2):
                if step_kind == SUBDIAG:
                    src[c][corner] = src[c][corner] + bias_ref[(0, 1) + corner]
                s = src[c][block]
                if step_kind == DIAG:
                    s = s + bias_ref[(0, 0) + block]
                m_old = m_refs[c][:, queries]
                m_new = jnp.maximum(m_old, jnp.max(s, axis=0, keepdims=True))
                m_refs[c][:, queries] = m_new
                p = jnp.exp2(s - m_new).astype(BF16)
                pv = jnp.dot(vt, p, preferred_element_type=F32)
                acc_refs[c][:, queries] = jnp.exp2(m_old - m_new) * acc_refs[c][:, queries] + pv

    def pair(t0, kinds):
        logits(t0 + 1, buf_b, kinds[1])
        absorb(t0, buf_a, kinds[0])
        if kinds[2] is not None:
            logits(t0 + 2, buf_a, kinds[2])
        absorb(t0 + 1, buf_b, kinds[1])

    def rolled(first, count):
        kinds = (kind(first), kind(first + 1), kind(first))

        def body(i, carry):
            pair(first + 2 * i, kinds)
            return carry
        if count > 0:
            lax.fori_loop(0, count, body, 0)

    logits(0, buf_a, kind(0))
    rolled(0, n_far // 2)
    t = 2 * (n_far // 2)
    if n_far % 2:
        pair(t, (kind(t), kind(t + 1), kind(t + 2)))
        t += 2
    rest = n_steps - t
    n_rolled = max(rest // 2 - (1 - rest % 2), 0)
    rolled(t, n_rolled)
    t += 2 * n_rolled
    if rest // 2 > n_rolled:
        pair(t, (kind(t), kind(t + 1), kind(t + 2)))
        t += 2
    if rest % 2:
        absorb(t, buf_a, kind(t))

    lp = lam_ref[...]
    lam = (jnp.exp(jnp.sum(lp[0:1] * lp[1:2], axis=-1, keepdims=True))
           - jnp.exp(jnp.sum(lp[2:3] * lp[3:4], axis=-1, keepdims=True)) + lambda_init)
    out_gain = sgc_ref[...] * (1.0 - lambda_init)

    def finish(c, carry):
        queries = pl.ds(pl.multiple_of(c * tq, tq), tq)
        a1 = acc1_ref[:, queries]
        a2 = acc2_ref[:, queries]
        ot = a1[:HEAD_W] / a1[HEAD_W:HEAD_W + 1] - lam * (a2[:HEAD_W] / a2[HEAD_W:HEAD_W + 1])
        ms = jnp.mean(ot * ot, axis=0, keepdims=True)
        ot = ot * lax.rsqrt(ms + EPS) * out_gain
        o_ref[0, queries, :] = ot.T.astype(o_ref.dtype)
        return carry

    lax.fori_loop(0, seq // tq, finish, 0)


def _attention(qkv, bias_tiles, q_gain2, k_gain2, lam_params, subln_gain_col, lambda_init):
    n_slabs, b, s, _ = qkv.shape
    n_heads = n_slabs // 3
    assert s % ATTN_TILE == 0
    tq = ATTN_TILE
    far, near = _attn_schedule(s // tq)
    steps = jnp.asarray(far + near, I32)

    def im(f):
        return lambda bi, h, sk, sq: f(bi, h)

    grid_spec = pltpu.PrefetchScalarGridSpec(
        num_scalar_prefetch=2,
        grid=(b, n_heads),
        in_specs=[pl.BlockSpec((None, 1, s, HEAD_W), im(lambda bi, h: (h, bi, 0, 0))),
                  pl.BlockSpec((None, 1, s, HEAD_W), im(lambda bi, h: (n_heads + h, bi, 0, 0))),
                  pl.BlockSpec((None, 1, s, HEAD_W), im(lambda bi, h: (2 * n_heads + h, bi, 0, 0))),
                  pl.BlockSpec((1, 2, tq, tq), im(lambda bi, h: (h, 0, 0, 0))),
                  pl.BlockSpec((HEAD_W, 1), im(lambda bi, h: (0, 0))),
                  pl.BlockSpec((HEAD_W, 1), im(lambda bi, h: (0, 0))),
                  pl.BlockSpec((4, HEAD_DIM), im(lambda bi, h: (0, 0))),
                  pl.BlockSpec((HEAD_W, 1), im(lambda bi, h: (0, 0)))],
        out_specs=pl.BlockSpec((None, 1, s, HEAD_W), im(lambda bi, h: (h, bi, 0, 0))),
        scratch_shapes=[pltpu.VMEM((s, HEAD_W), BF16),
                        pltpu.VMEM((VT_ROWS, s), BF16)]
        + [pltpu.VMEM((HEAD_W, s), BF16)] * 2
        + [pltpu.VMEM((tq, tq), F32)] * 4
        + [pltpu.VMEM((VT_ROWS, s), F32)] * 2
        + [pltpu.VMEM((1, s), F32)] * 2,
    )
    return pl.pallas_call(
        functools.partial(_attn_kernel, lambda_init=lambda_init, n_far=len(far), n_steps=len(far) + len(near)),
        grid_spec=grid_spec,
        out_shape=jax.ShapeDtypeStruct((n_heads, b, s, HEAD_W), BF16),
        compiler_params=_params("parallel", "parallel"),
        name="diff_attention",
    )(steps[:, 0], steps[:, 1], qkv, qkv, qkv, bias_tiles, q_gain2, k_gain2, lam_params,
      subln_gain_col)


def _pool_kernel(x_ref, g_ref, win_ref, wgrp_ref, wout_ref, scale_ref, o_ref, ext_ref, pooled_ref):
    j = pl.program_id(1)
    tm = x_ref.shape[1]
    d = x_ref.shape[2]
    gw = d // len(POOL_WINDOWS)
    x = x_ref[0]

    @pl.when(j == 0)
    def _():
        ext_ref[0:POOL_HALO, :] = jnp.zeros((POOL_HALO, d), F32)

    @pl.when(j > 0)
    def _():
        ext_ref[0:POOL_HALO, :] = ext_ref[tm:tm + POOL_HALO, :]

    h = _rms(x, g_ref[...]).astype(BF16)
    ext_ref[POOL_HALO:POOL_HALO + tm, :] = jnp.dot(h, win_ref[...], preferred_element_type=F32)

    pos1 = (j * tm + 1 + lax.broadcasted_iota(I32, (tm, 1), 0)).astype(F32)
    for g, win in enumerate(POOL_WINDOWS):
        c0, c1 = g * gw, (g + 1) * gw
        u = ext_ref[POOL_HALO:POOL_HALO + tm, c0:c1]
        s = u
        for k in range(1, win):
            s = s + ext_ref[POOL_HALO - k:POOL_HALO - k + tm, c0:c1]
        inv_cnt = 1.0 / jnp.minimum(pos1, float(win))
        pooled = (s * inv_cnt - u).astype(BF16)
        pooled_ref[:, c0:c1] = jnp.dot(pooled, wgrp_ref[g], preferred_element_type=F32).astype(BF16)

    y = jnp.dot(pooled_ref[...], wout_ref[...], preferred_element_type=F32)
    o_ref[0] = x + y * scale_ref[...]


def _pool_mixer(x, gain, w_in, w_group, w_out, scale):
    b, s, d = x.shape
    tm = min(ROW_TILE, s)
    ng = len(POOL_WINDOWS)
    gw = d // ng
    return pl.pallas_call(
        _pool_kernel,
        grid=(b, s // tm),
        in_specs=[pl.BlockSpec((1, tm, d), lambda bi, j: (bi, j, 0)),
                  pl.BlockSpec((1, d), lambda bi, j: (0, 0)),
                  pl.BlockSpec((d, d), lambda bi, j: (0, 0)),
                  pl.BlockSpec((ng, gw, gw), lambda bi, j: (0, 0, 0)),
                  pl.BlockSpec((d, d), lambda bi, j: (0, 0)),
                  pl.BlockSpec((1, d), lambda bi, j: (0, 0))],
        out_specs=pl.BlockSpec((1, tm, d), lambda bi, j: (bi, j, 0)),
        out_shape=jax.ShapeDtypeStruct((b, s, d), F32),
        scratch_shapes=[pltpu.VMEM((POOL_HALO + tm, d), F32),
                        pltpu.VMEM((tm, d), BF16)],
        compiler_params=_params("arbitrary", "arbitrary"),
        name="pool_mixer",
    )(x, gain, w_in, w_group, w_out, scale)


def _router_kernel(x_ref, g_ref, wr_ref, br_ref, o_ref, cnt_ref, carry_ref):
    tm = x_ref.shape[0]

    @pl.when(pl.program_id(0) == 0)
    def _():
        carry_ref[...] = jnp.zeros(carry_ref.shape, F32)

    h = _rms(x_ref[...], g_ref[...])
    h_hi = h.astype(BF16)
    h_lo = (h - h_hi.astype(F32)).astype(BF16)
    w = wr_ref[...]
    half = ROUTER_COLS // 2
    nt = (((1,), (1,)), ((), ()))
    a = lax.dot_general(w, h_hi, nt, preferred_element_type=F32)
    b = lax.dot_general(w, h_lo, nt, preferred_element_type=F32)
    lt = a[:half] + a[half:] + b[:half] + br_ref[...]

    def row(r):
        return lt[r:r + 1, :]

    g_best = row(N_EXPERTS)
    g_idx = jnp.zeros((1, tm), I32)
    for g in range(1, N_EXPERT_GROUPS):
        v = row(N_EXPERTS + g)
        better = v > g_best
        g_best = jnp.where(better, v, g_best)
        g_idx = jnp.where(better, g, g_idx)
    denom = jnp.zeros((1, tm), F32)
    for g in range(N_EXPERT_GROUPS):
        denom = denom + jnp.exp(row(N_EXPERTS + g) - g_best)
    g_prob = 1.0 / denom

    sel = []
    for e in range(EXPERTS_PER_GROUP):
        v = row(e)
        for g in range(1, N_EXPERT_GROUPS):
            v = jnp.where(g_idx == g, row(g * EXPERTS_PER_GROUP + e), v)
        sel.append(v)

    def top1(vals):
        best, idx = vals[0], jnp.zeros((1, tm), I32)
        for e in range(1, EXPERTS_PER_GROUP):
            better = vals[e] > best
            best = jnp.where(better, vals[e], best)
            idx = jnp.where(better, e, idx)
        return best, idx

    v0, i0 = top1(sel)
    v1, i1 = top1([jnp.where(i0 == e, -jnp.inf, sel[e]) for e in range(EXPERTS_PER_GROUP)])
    t = jnp.exp(v1 - v0)
    w0 = g_prob / (1.0 + t)
    w1 = w0 * t
    e0 = g_idx * EXPERTS_PER_GROUP + i0
    e1 = g_idx * EXPERTS_PER_GROUP + i1

    eid = lax.broadcasted_iota(I32, (N_EXPERTS, tm), 0)
    oh0 = eid == e0
    oh1 = eid == e1
    onehot = jnp.where(oh0 | oh1, 1.0, 0.0)
    earlier = (lax.broadcasted_iota(I32, (tm, tm), 0) < lax.broadcasted_iota(I32, (tm, tm), 1))
    prefix = jnp.dot(onehot.astype(BF16), jnp.where(earlier, 1.0, 0.0).astype(BF16),
                     preferred_element_type=F32)
    before = prefix + carry_ref[:, 0:1]
    rank0 = jnp.sum(jnp.where(oh0, before, 0.0), axis=0, keepdims=True)
    rank1 = jnp.sum(jnp.where(oh1, before, 0.0), axis=0, keepdims=True)
    carry_ref[...] = carry_ref[...] + jnp.sum(onehot, axis=1, keepdims=True)
    cnt_ref[...] = carry_ref[...]

    o_ref[...] = jnp.concatenate(
        [e0.astype(F32), e1.astype(F32), w0, w1, rank0, rank1, jnp.zeros((2, tm), F32)], axis=0)


def _router(x2, gain, wr, br):
    t, d = x2.shape
    tm = min(ROW_TILE, t)
    assert t * TOP_K < 2 ** 24
    return pl.pallas_call(
        _router_kernel,
        grid=(t // tm,),
        in_specs=[pl.BlockSpec((tm, d), lambda i: (i, 0)),
                  pl.BlockSpec((1, d), lambda i: (0, 0)),
                  pl.BlockSpec((ROUTER_COLS, d), lambda i: (0, 0)),
                  pl.BlockSpec((ROUTER_COLS // 2, 1), lambda i: (0, 0))],
        out_specs=[pl.BlockSpec((8, tm), lambda i: (0, i)),
                   pl.BlockSpec((N_EXPERTS, LANES), lambda i: (0, 0))],
        out_shape=[jax.ShapeDtypeStruct((8, t), F32),
                   jax.ShapeDtypeStruct((N_EXPERTS, LANES), F32)],
        scratch_shapes=[pltpu.VMEM((N_EXPERTS, LANES), F32)],
        compiler_params=_params("arbitrary"),
        name="router",
    )(x2, gain, wr, br)


def _slot_positions(route, counts_f, blk, n_blocks):
    e = route[0:TOP_K].astype(I32).T.reshape(-1)
    rank = route[2 * TOP_K:3 * TOP_K].astype(I32).T.reshape(-1)
    counts = counts_f[:, 0].astype(I32)
    padded = (counts + blk - 1) // blk * blk
    pad_end = jnp.cumsum(padded)
    pad_start = pad_end - padded
    experts = jnp.arange(N_EXPERTS, dtype=I32)
    pos = rank + jnp.sum(jnp.where(e[:, None] == experts[None, :], pad_start[None, :], 0), axis=1)
    first_row = jnp.arange(n_blocks, dtype=I32) * blk
    block_expert = jnp.minimum(
        jnp.sum((pad_end[None, :] <= first_row[:, None]).astype(I32), axis=1), N_EXPERTS - 1)
    n_used = (pad_end[-1] // blk).reshape(1)
    return pos.astype(I32), block_expert.astype(I32), n_used.astype(I32)


def _row_copy(src_ref, src_row, dst_ref, dst_row, sem):
    return pltpu.make_async_copy(src_ref.at[src_row], dst_ref.at[dst_row], sem)


def _dispatch_kernel(pos_ref, x_ref, g_ref, xs_in_ref, xs_ref, buf_ref, sem):
    del xs_in_ref
    i = pl.program_id(0)
    n = pl.num_programs(0)
    tm = x_ref.shape[0]
    slot = i % 2

    def wait_slot(sl):
        for _ in range(TOP_K):
            pltpu.make_async_copy(buf_ref.at[sl], xs_ref.at[pl.ds(0, tm)], sem.at[sl]).wait()

    @pl.when(i >= 2)
    def _():
        wait_slot(slot)

    packed = _pack_rows(_rms(x_ref[...], g_ref[...]))
    buf_ref[slot] = packed.reshape(tm, packed.shape[1] // LANES, LANES)

    def issue(g, c):
        base = g * ISSUE_UNROLL
        for u in range(ISSUE_UNROLL):
            for k in range(TOP_K):
                p = pos_ref[(i * tm + base + u) * TOP_K + k]
                _row_copy(buf_ref.at[slot], base + u, xs_ref, p, sem.at[slot]).start()
        return c

    lax.fori_loop(0, tm // ISSUE_UNROLL, issue, 0)

    @pl.when(i == n - 1)
    def _():
        wait_slot(slot)

    @pl.when((i == n - 1) & (n >= 2))
    def _():
        wait_slot(1 - slot)


def _dispatch(pos, x2, gain, n_slots):
    t, d = x2.shape
    tm = min(MOVE_TILE, t)
    chunks = d // 2 // LANES
    zeros = jnp.zeros((n_slots, chunks, LANES), U32)
    grid_spec = pltpu.PrefetchScalarGridSpec(
        num_scalar_prefetch=1,
        grid=(t // tm,),
        in_specs=[pl.BlockSpec((tm, d), lambda i, pos: (i, 0)),
                  pl.BlockSpec((1, d), lambda i, pos: (0, 0)),
                  pl.BlockSpec(memory_space=pl.ANY)],
        out_specs=pl.BlockSpec(memory_space=pl.ANY),
        scratch_shapes=[pltpu.VMEM((2, tm, chunks, LANES), U32),
                        pltpu.SemaphoreType.DMA((2,))],
    )
    slabs = pl.pallas_call(
        _dispatch_kernel,
        grid_spec=grid_spec,
        out_shape=jax.ShapeDtypeStruct((n_slots, chunks, LANES), U32),
        input_output_aliases={3: 0},
        compiler_params=_params("arbitrary"),
        name="dispatch",
    )(pos, x2, gain, zeros)
    return slabs


def _expert_kernel(be_ref, nu_ref, xs_ref, wg_ref, wu_ref, wd_ref, y_ref, wgb_ref, wub_ref, wdb_ref):
    i = pl.program_id(0)
    e = be_ref[i]
    prev = be_ref[jnp.maximum(i - 1, 0)]

    @pl.when((i == 0) | (e != prev))
    def _():
        wgb_ref[...] = wg_ref[0, 0].astype(BF16)
        wub_ref[...] = wu_ref[0, 0].astype(BF16)
        wdb_ref[...] = wd_ref[0, 0].astype(BF16)

    @pl.when(i < nu_ref[0])
    def _():
        blk, chunks, lanes = xs_ref.shape
        x = _unpack_rows(xs_ref[...].reshape(blk, chunks * lanes)).astype(BF16)
        g = jnp.dot(x, wgb_ref[...], preferred_element_type=F32)
        u = jnp.dot(x, wub_ref[...], preferred_element_type=F32)
        a = (g / (1.0 + jnp.exp(-g)) * u).astype(BF16)
        y = _pack_rows(jnp.dot(a, wdb_ref[...], preferred_element_type=F32))
        y_ref[...] = y.reshape(blk, chunks, lanes)

    @pl.when(i >= nu_ref[0])
    def _():
        y_ref[...] = jnp.zeros(y_ref.shape, U32)


def _experts(block_expert, n_used, xs, w_gate, w_up, w_down, layer):
    n_slots, chunks, lanes = xs.shape
    blk = EXPERT_BLOCK
    d, ff = w_gate.shape[2], w_gate.shape[3]
    grid_spec = pltpu.PrefetchScalarGridSpec(
        num_scalar_prefetch=2,
        grid=(n_slots // blk,),
        in_specs=[pl.BlockSpec((blk, chunks, lanes), lambda i, be, nu: (i, 0, 0)),
                  pl.BlockSpec((1, 1, d, ff), lambda i, be, nu: (layer, be[i], 0, 0)),
                  pl.BlockSpec((1, 1, d, ff), lambda i, be, nu: (layer, be[i], 0, 0)),
                  pl.BlockSpec((1, 1, ff, d), lambda i, be, nu: (layer, be[i], 0, 0))],
        out_specs=pl.BlockSpec((blk, chunks, lanes), lambda i, be, nu: (i, 0, 0)),
        scratch_shapes=[pltpu.VMEM((d, ff), BF16),
                        pltpu.VMEM((d, ff), BF16),
                        pltpu.VMEM((ff, d), BF16)],
    )
    return pl.pallas_call(
        _expert_kernel,
        grid_spec=grid_spec,
        out_shape=jax.ShapeDtypeStruct((n_slots, chunks, lanes), U32),
        compiler_params=_params("arbitrary"),
        name="experts",
    )(block_expert, n_used, xs, w_gate, w_up, w_down)


def _combine_kernel(pos_ref, y_ref, x_ref, w_ref, o_ref, buf_ref, sem):
    i = pl.program_id(0)
    n = pl.num_programs(0)
    tm = x_ref.shape[0]
    slot = i % 2

    def issue_tile(tile, sl):
        def body(g, c):
            base = pl.multiple_of(g * ISSUE_UNROLL, ISSUE_UNROLL)
            for u in range(ISSUE_UNROLL):
                for k in range(TOP_K):
                    p = pos_ref[(tile * tm + base + u) * TOP_K + k]
                    _row_copy(y_ref, p, buf_ref.at[sl], k * tm + base + u, sem.at[sl]).start()
            return c
        lax.fori_loop(0, tm // ISSUE_UNROLL, body, 0)

    @pl.when(i == 0)
    def _():
        issue_tile(0, 0)

    @pl.when(i + 1 < n)
    def _():
        issue_tile(i + 1, 1 - slot)

    pltpu.make_async_copy(y_ref.at[pl.ds(0, TOP_K * tm)], buf_ref.at[slot], sem.at[slot]).wait()

    slabs = buf_ref[slot]
    rows = slabs.reshape(slabs.shape[0], slabs.shape[1] * slabs.shape[2])
    w = w_ref[...]
    out = x_ref[...]
    for k in range(TOP_K):
        out = out + w[:, k:k + 1] * _unpack_rows(rows[k * tm:(k + 1) * tm])
    o_ref[...] = out


def _combine(pos, y, x2, weights):
    t, d = x2.shape
    tm = min(MOVE_TILE, t)
    chunks = d // 2 // LANES
    grid_spec = pltpu.PrefetchScalarGridSpec(
        num_scalar_prefetch=1,
        grid=(t // tm,),
        in_specs=[pl.BlockSpec(memory_space=pl.ANY),
                  pl.BlockSpec((tm, d), lambda i, pos: (i, 0)),
                  pl.BlockSpec((tm, TOP_K), lambda i, pos: (i, 0))],
        out_specs=pl.BlockSpec((tm, d), lambda i, pos: (i, 0)),
        scratch_shapes=[pltpu.VMEM((2, TOP_K * tm, chunks, LANES), U32),
                        pltpu.SemaphoreType.DMA((2,))],
    )
    return pl.pallas_call(
        _combine_kernel,
        grid_spec=grid_spec,
        out_shape=jax.ShapeDtypeStruct((t, d), F32),
        compiler_params=_params("arbitrary"),
        name="combine",
    )(pos, y, x2, weights)


def _hier_moe(x2, gain, wg1, bg1, wg2, bg2, w_gate, w_up, w_down, layer):
    t, d = x2.shape
    half = ROUTER_COLS // 2
    pad = half - N_EXPERTS - N_EXPERT_GROUPS
    w_f32 = jnp.concatenate([wg2.astype(F32), wg1.astype(F32), jnp.zeros((d, pad), F32)], axis=1)
    w_hi = w_f32.astype(BF16)
    w_lo = (w_f32 - w_hi.astype(F32)).astype(BF16)
    wr = jnp.concatenate([w_hi, w_lo], axis=1).T
    br = jnp.concatenate([bg2.astype(F32), bg1.astype(F32), jnp.zeros((pad,), F32)]).reshape(half, 1)
    route, counts = _router(x2, gain, wr, br)

    blk = EXPERT_BLOCK
    n_blocks = -(-(t * TOP_K) // blk) + N_EXPERTS
    pos, block_expert, n_used = _slot_positions(route, counts, blk, n_blocks)
    xs = _dispatch(pos, x2, gain, n_blocks * blk)
    y = _experts(block_expert, n_used, xs, w_gate, w_up, w_down, layer)
    return _combine(pos, y, x2, route[TOP_K:2 * TOP_K].T)


def kernel(x, rel_bias, attn_norm, w_qkv, q_gain, k_gain, lambda_q1, lambda_k1, lambda_q2, lambda_k2,
           subln_gain, w_o, pool_norm, pool_w_in, pool_w_group, pool_w_out, pool_scale,
           ffn_norm, router_group_w, router_group_b, router_expert_w, router_expert_b,
           w_gate, w_up, w_down):
    b, s, d = x.shape
    depth = ffn_norm.shape[0]
    x = x.astype(F32)
    bias_tiles = _bias_tiles(rel_bias)
    for i in range(depth):
        j = i // N_MIXERS
        if i % N_MIXERS == 0:
            lambda_init = 0.8 - 0.6 * math.exp(-0.3 * i)
            x2 = x.reshape(b * s, d)
            qkv = _norm_matmul(x2, attn_norm[j].reshape(1, d).astype(F32), w_qkv[j].astype(BF16))
            lam_params = jnp.stack([lambda_q1[j], lambda_k1[j], lambda_q2[j], lambda_k2[j]]).astype(F32)
            o = _attention(qkv.reshape(-1, b, s, HEAD_W), bias_tiles,
                           jnp.tile(q_gain[j].astype(F32), 2).reshape(HEAD_W, 1),
                           jnp.tile(k_gain[j].astype(F32), 2).reshape(HEAD_W, 1),
                           lam_params, subln_gain[j].reshape(HEAD_W, 1).astype(F32), lambda_init)
            x2 = _matmul_residual(o.reshape(-1, b * s, HEAD_W), w_o[j].astype(BF16), x2)
        else:
            x3 = _pool_mixer(x, pool_norm[j].reshape(1, d).astype(F32), pool_w_in[j].astype(BF16),
                             pool_w_group[j].astype(BF16), pool_w_out[j].astype(BF16),
                             pool_scale[j].reshape(1, d).astype(F32))
            x2 = x3.reshape(b * s, d)
        x2 = _hier_moe(x2, ffn_norm[i].reshape(1, d).astype(F32), router_group_w[i], router_group_b[i],
                       router_expert_w[i], router_expert_b[i], w_gate, w_up, w_down, i)
        x = x2.reshape(b, s, d)
    return x
```

```python
import functools
import math

import jax
import jax.numpy as jnp
from jax import lax
from jax.experimental import pallas as pl
from jax.experimental.pallas import tpu as pltpu

F32 = jnp.float32
BF16 = jnp.bfloat16
U32 = jnp.uint32
I32 = jnp.int32

EPS = 1e-6
HEAD_DIM = 64
HEAD_W = 2 * HEAD_DIM
NUM_BUCKETS = 32
MAX_EXACT = NUM_BUCKETS // 2
MAX_DISTANCE = 128
POOL_WINDOWS = (2, 4, 8, 16)
N_EXPERT_GROUPS = 4
EXPERTS_PER_GROUP = 8
N_EXPERTS = N_EXPERT_GROUPS * EXPERTS_PER_GROUP
TOP_K = 2
N_MIXERS = 2
LOG2E = math.log2(math.e)

LANES = 128
VMEM_LIMIT = 48 * 1024 * 1024
ROW_TILE = 512
ATTN_TILE = 512
ATTN_PREP_ROWS = 256
VT_ROWS = HEAD_W + 16
EXPERT_BLOCK = 512
MOVE_TILE = 256
ISSUE_UNROLL = 16
POOL_HALO = 16
MASK_VALUE = -1e30
FAR, SUBDIAG, DIAG = 0, 1, 2
ROUTER_COLS = LANES


def _params(*sem):
    return pltpu.CompilerParams(dimension_semantics=sem, vmem_limit_bytes=VMEM_LIMIT)


def _rms(x, gain):
    ms = jnp.mean(x * x, axis=-1, keepdims=True)
    return x * lax.rsqrt(ms + EPS) * gain


def _pack_rows(v):
    c = v.shape[1] // 2
    bits = lax.bitcast_convert_type(v.astype(BF16).astype(F32), U32)
    return (bits[:, :c] >> 16) | (bits[:, c:] & jnp.uint32(0xFFFF0000))


def _unpack_rows(w):
    lo = lax.bitcast_convert_type(w << 16, F32)
    hi = lax.bitcast_convert_type(w & jnp.uint32(0xFFFF0000), F32)
    return jnp.concatenate([lo, hi], axis=1)


def _norm_matmul_kernel(x_ref, g_ref, w_ref, o_ref, *, n_chunks):
    h = _rms(x_ref[...], g_ref[...]).astype(BF16)
    cw = w_ref.shape[1] // n_chunks
    for c in range(n_chunks):
        r = jnp.dot(h, w_ref[:, c * cw:(c + 1) * cw], preferred_element_type=F32).astype(o_ref.dtype)
        for j in range(cw // HEAD_W):
            o_ref[c * (cw // HEAD_W) + j] = r[:, j * HEAD_W:(j + 1) * HEAD_W]


def _norm_matmul(x2, gain, w):
    t, d = x2.shape
    n = w.shape[1]
    tm = min(ROW_TILE, t)
    return pl.pallas_call(
        functools.partial(_norm_matmul_kernel, n_chunks=n // d),
        grid=(t // tm,),
        in_specs=[pl.BlockSpec((tm, d), lambda i: (i, 0)),
                  pl.BlockSpec((1, d), lambda i: (0, 0)),
                  pl.BlockSpec((d, n), lambda i: (0, 0))],
        out_specs=pl.BlockSpec((n // HEAD_W, tm, HEAD_W), lambda i: (0, i, 0)),
        out_shape=jax.ShapeDtypeStruct((n // HEAD_W, t, HEAD_W), BF16),
        compiler_params=_params("parallel"),
        name="norm_matmul",
    )(x2, gain, w)


def _matmul_residual_kernel(a_ref, w_ref, x_ref, o_ref):
    a = jnp.concatenate([a_ref[h] for h in range(a_ref.shape[0])], axis=1)
    o_ref[...] = x_ref[...] + jnp.dot(a, w_ref[...], preferred_element_type=F32)


def _matmul_residual(a, w, x2):
    t, d = x2.shape
    k = w.shape[0]
    tm = min(ROW_TILE, t)
    return pl.pallas_call(
        _matmul_residual_kernel,
        grid=(t // tm,),
        in_specs=[pl.BlockSpec((k // HEAD_W, tm, HEAD_W), lambda i: (0, i, 0)),
                  pl.BlockSpec((k, d), lambda i: (0, 0)),
                  pl.BlockSpec((tm, d), lambda i: (i, 0))],
        out_specs=pl.BlockSpec((tm, d), lambda i: (i, 0)),
        out_shape=jax.ShapeDtypeStruct((t, d), F32),
        compiler_params=_params("parallel"),
        name="matmul_residual",
    )(a, w, x2)


def _rel_bucket(dist):
    n = jnp.maximum(dist, 0)
    nf = jnp.maximum(n, 1).astype(F32)
    large = MAX_EXACT + (jnp.log(nf / MAX_EXACT) / math.log(MAX_DISTANCE / MAX_EXACT)
                         * (NUM_BUCKETS - MAX_EXACT)).astype(I32)
    large = jnp.minimum(large, NUM_BUCKETS - 1)
    return jnp.where(n < MAX_EXACT, n, large)


def _bias_tiles(rel_bias):
    assert ATTN_TILE >= MAX_DISTANCE
    kj = jnp.arange(ATTN_TILE)[:, None]
    qi = jnp.arange(ATTN_TILE)[None, :]
    table = rel_bias.astype(F32)
    vals = (table - table[NUM_BUCKETS - 1]) * LOG2E
    tiles = []
    for off in (0, ATTN_TILE):
        dist = qi - kj + off
        bucket = _rel_bucket(dist)[None]
        b = jnp.zeros((table.shape[1], ATTN_TILE, ATTN_TILE), F32)
        for n in range(NUM_BUCKETS):
            b = jnp.where(bucket == n, vals[n][:, None, None], b)
        tiles.append(jnp.where((dist >= 0)[None], b, MASK_VALUE))
    return jnp.stack(tiles, axis=1)


def _attn_schedule(n_q_tiles):
    far = [(kj, qi, 0) for qi in range(n_q_tiles) for kj in range(qi - 1)]
    near = []
    for qi in range(n_q_tiles):
        if qi >= 1:
            near.append((qi - 1, qi, 1))
        near.append((qi, qi, 0))
    return far, near


def _attn_kernel(sk_ref, sq_ref, q_ref, k_ref, v_ref, bias_ref, qgc_ref, kgc_ref, lam_ref, sgc_ref,
                 o_ref, kn_ref, vt_ref, qpt1_ref, qpt2_ref, sa1_ref, sa2_ref, sb1_ref, sb2_ref,
                 acc1_ref, acc2_ref, m1_ref, m2_ref, *, lambda_init, n_far, n_steps):
    qpt_refs, acc_refs, m_refs = (qpt1_ref, qpt2_ref), (acc1_ref, acc2_ref), (m1_ref, m2_ref)
    buf_a, buf_b = (sa1_ref, sa2_ref), (sb1_ref, sb2_ref)
    seq = k_ref.shape[1]
    tq = ATTN_TILE
    pr = ATTN_PREP_ROWS
    nt = (((1,), (1,)), ((), ()))

    def identity(n):
        return jnp.where(lax.broadcasted_iota(I32, (n, n), 0) == lax.broadcasted_iota(I32, (n, n), 1),
                         1.0, 0.0).astype(BF16)

    eye_w, eye_rows = identity(HEAD_W), identity(pr)
    first_rows = lax.broadcasted_iota(I32, (HEAD_W, 1), 0) < HEAD_DIM

    def transposed(eye, x):
        return lax.dot_general(eye, x, nt, preferred_element_type=F32)

    def half_norm_t(xt, gain_col):
        sq = xt * xt
        r1 = lax.rsqrt(jnp.sum(sq[:HEAD_DIM], axis=0, keepdims=True) / HEAD_DIM + EPS)
        r2 = lax.rsqrt(jnp.sum(sq[HEAD_DIM:], axis=0, keepdims=True) / HEAD_DIM + EPS)
        return xt * jnp.where(first_rows, r1, r2) * gain_col

    def prep(c, carry):
        r0 = pl.multiple_of(c * pr, pr)
        rows = pl.ds(r0, pr)
        vt_ref[0:HEAD_W, rows] = transposed(eye_w, v_ref[0, rows, :]).astype(BF16)
        knt = half_norm_t(transposed(eye_w, k_ref[0, rows, :]), kgc_ref[...]).astype(BF16)
        kn_ref[rows, :] = transposed(eye_rows, knt).astype(BF16)
        qnt = half_norm_t(transposed(eye_w, q_ref[0, rows, :]), qgc_ref[...]) * (HEAD_DIM ** -0.5 * LOG2E)
        qpt1_ref[:, rows] = jnp.where(first_rows, qnt, 0.0).astype(BF16)
        qpt2_ref[:, rows] = jnp.where(first_rows, 0.0, qnt).astype(BF16)
        return carry

    lax.fori_loop(0, seq // pr, prep, 0, unroll=4)
    ones_row = lax.broadcasted_iota(I32, (VT_ROWS - HEAD_W, seq), 0) == 0
    vt_ref[HEAD_W:VT_ROWS, :] = jnp.where(ones_row, 1.0, 0.0).astype(BF16)
    for c in range(2):
        m_refs[c][...] = jnp.full(m_refs[c].shape, MASK_VALUE, F32)
        acc_refs[c][...] = jnp.zeros(acc_refs[c].shape, F32)

    def kind(t):
        if t >= n_steps:
            return None
        if t < n_far:
            return FAR
        return DIAG if (t - n_far) % 2 == 0 else SUBDIAG

    half = tq // 2

    def parts(step_kind):
        if step_kind == DIAG:
            return (((0, half), (0, half)), ((0, tq), (half, tq)))
        return (((0, tq), (0, tq)),)

    def span(base, lo_hi):
        lo, hi = lo_hi
        return pl.ds(pl.multiple_of(base + lo, half), hi - lo)

    def bases(t):
        return pl.multiple_of(sk_ref[t] * tq, tq), pl.multiple_of(sq_ref[t] * tq, tq)

    def logits(t, dst, step_kind):
        k0, q0 = bases(t)
        for kr, qc in parts(step_kind):
            kt = kn_ref[span(k0, kr), :]
            for c in range(2):
                dst[c][kr[0]:kr[1], qc[0]:qc[1]] = jnp.dot(
                    kt, qpt_refs[c][:, span(q0, qc)], preferred_element_type=F32)

    corner = (slice(tq - MAX_DISTANCE, tq), slice(0, MAX_DISTANCE))

    def absorb(t, src, step_kind):
        k0, q0 = bases(t)
        for kr, qc in parts(step_kind):
            block = (slice(kr[0], kr[1]), slice(qc[0], qc[1]))
            queries = span(q0, qc)
            vt = vt_ref[:, span(k0, kr)]
            for c in range(2):
                if step_kind == SUBDIAG:
                    src[c][corner] = src[c][corner] + bias_ref[(0, 1) + corner]
                s = src[c][block]
                if step_kind == DIAG:
                    s = s + bias_ref[(0, 0) + block]
                m_old = m_refs[c][:, queries]
                m_new = jnp.maximum(m_old, jnp.max(s, axis=0, keepdims=True))
                m_refs[c][:, queries] = m_new
                p = jnp.exp2(s - m_new).astype(BF16)
                pv = jnp.dot(vt, p, preferred_element_type=F32)
                acc_refs[c][:, queries] = jnp.exp2(m_old - m_new) * acc_refs[c][:, queries] + pv

    def pair(t0, kinds):
        logits(t0 + 1, buf_b, kinds[1])
        absorb(t0, buf_a, kinds[0])
        if kinds[2] is not None:
            logits(t0 + 2, buf_a, kinds[2])
        absorb(t0 + 1, buf_b, kinds[1])

    def rolled(first, count):
        kinds = (kind(first), kind(first + 1), kind(first))

        def body(i, carry):
            pair(first + 2 * i, kinds)
            return carry
        if count > 0:
            lax.fori_loop(0, count, body, 0)

    logits(0, buf_a, kind(0))
    rolled(0, n_far // 2)
    t = 2 * (n_far // 2)
    if n_far % 2:
        pair(t, (kind(t), kind(t + 1), kind(t + 2)))
        t += 2
    rest = n_steps - t
    n_rolled = max(rest // 2 - (1 - rest % 2), 0)
    rolled(t, n_rolled)
    t += 2 * n_rolled
    if rest // 2 > n_rolled:
        pair(t, (kind(t), kind(t + 1), kind(t + 2)))
        t += 2
    if rest % 2:
        absorb(t, buf_a, kind(t))

    lp = lam_ref[...]
    lam = (jnp.exp(jnp.sum(lp[0:1] * lp[1:2], axis=-1, keepdims=True))
           - jnp.exp(jnp.sum(lp[2:3] * lp[3:4], axis=-1, keepdims=True)) + lambda_init)
    out_gain = sgc_ref[...] * (1.0 - lambda_init)

    def finish(c, carry):
        queries = pl.ds(pl.multiple_of(c * tq, tq), tq)
        a1 = acc1_ref[:, queries]
        a2 = acc2_ref[:, queries]
        ot = a1[:HEAD_W] / a1[HEAD_W:HEAD_W + 1] - lam * (a2[:HEAD_W] / a2[HEAD_W:HEAD_W + 1])
        ms = jnp.mean(ot * ot, axis=0, keepdims=True)
        ot = ot * lax.rsqrt(ms + EPS) * out_gain
        o_ref[0, queries, :] = ot.T.astype(o_ref.dtype)
        return carry

    lax.fori_loop(0, seq // tq, finish, 0)


def _attention(qkv, bias_tiles, q_gain2, k_gain2, lam_params, subln_gain_col, lambda_init):
    n_slabs, b, s, _ = qkv.shape
    n_heads = n_slabs // 3
    assert s % ATTN_TILE == 0
    tq = ATTN_TILE
    far, near = _attn_schedule(s // tq)
    steps = jnp.asarray(far + near, I32)

    def im(f):
        return lambda bi, h, sk, sq: f(bi, h)

    grid_spec = pltpu.PrefetchScalarGridSpec(
        num_scalar_prefetch=2,
        grid=(b, n_heads),
        in_specs=[pl.BlockSpec((None, 1, s, HEAD_W), im(lambda bi, h: (h, bi, 0, 0))),
                  pl.BlockSpec((None, 1, s, HEAD_W), im(lambda bi, h: (n_heads + h, bi, 0, 0))),
                  pl.BlockSpec((None, 1, s, HEAD_W), im(lambda bi, h: (2 * n_heads + h, bi, 0, 0))),
                  pl.BlockSpec((1, 2, tq, tq), im(lambda bi, h: (h, 0, 0, 0))),
                  pl.BlockSpec((HEAD_W, 1), im(lambda bi, h: (0, 0))),
                  pl.BlockSpec((HEAD_W, 1), im(lambda bi, h: (0, 0))),
                  pl.BlockSpec((4, HEAD_DIM), im(lambda bi, h: (0, 0))),
                  pl.BlockSpec((HEAD_W, 1), im(lambda bi, h: (0, 0)))],
        out_specs=pl.BlockSpec((None, 1, s, HEAD_W), im(lambda bi, h: (h, bi, 0, 0))),
        scratch_shapes=[pltpu.VMEM((s, HEAD_W), BF16),
                        pltpu.VMEM((VT_ROWS, s), BF16)]
        + [pltpu.VMEM((HEAD_W, s), BF16)] * 2
        + [pltpu.VMEM((tq, tq), F32)] * 4
        + [pltpu.VMEM((VT_ROWS, s), F32)] * 2
        + [pltpu.VMEM((1, s), F32)] * 2,
    )
    return pl.pallas_call(
        functools.partial(_attn_kernel, lambda_init=lambda_init, n_far=len(far), n_steps=len(far) + len(near)),
        grid_spec=grid_spec,
        out_shape=jax.ShapeDtypeStruct((n_heads, b, s, HEAD_W), BF16),
        compiler_params=_params("parallel", "parallel"),
        name="diff_attention",
    )(steps[:, 0], steps[:, 1], qkv, qkv, qkv, bias_tiles, q_gain2, k_gain2, lam_params,
      subln_gain_col)


def _pool_kernel(x_ref, g_ref, win_ref, wgrp_ref, wout_ref, scale_ref, o_ref, ext_ref, pooled_ref):
    j = pl.program_id(1)
    tm = x_ref.shape[1]
    d = x_ref.shape[2]
    gw = d // len(POOL_WINDOWS)
    x = x_ref[0]

    @pl.when(j == 0)
    def _():
        ext_ref[0:POOL_HALO, :] = jnp.zeros((POOL_HALO, d), F32)

    @pl.when(j > 0)
    def _():
        ext_ref[0:POOL_HALO, :] = ext_ref[tm:tm + POOL_HALO, :]

    h = _rms(x, g_ref[...]).astype(BF16)
    ext_ref[POOL_HALO:POOL_HALO + tm, :] = jnp.dot(h, win_ref[...], preferred_element_type=F32)

    pos1 = (j * tm + 1 + lax.broadcasted_iota(I32, (tm, 1), 0)).astype(F32)
    for g, win in enumerate(POOL_WINDOWS):
        c0, c1 = g * gw, (g + 1) * gw
        u = ext_ref[POOL_HALO:POOL_HALO + tm, c0:c1]
        s = u
        for k in range(1, win):
            s = s + ext_ref[POOL_HALO - k:POOL_HALO - k + tm, c0:c1]
        inv_cnt = 1.0 / jnp.minimum(pos1, float(win))
        pooled = (s * inv_cnt - u).astype(BF16)
        pooled_ref[:, c0:c1] = jnp.dot(pooled, wgrp_ref[g], preferred_element_type=F32).astype(BF16)

    y = jnp.dot(pooled_ref[...], wout_ref[...], preferred_element_type=F32)
    o_ref[0] = x + y * scale_ref[...]


def _pool_mixer(x, gain, w_in, w_group, w_out, scale):
    b, s, d = x.shape
    tm = min(ROW_TILE, s)
    ng = len(POOL_WINDOWS)
    gw = d // ng
    return pl.pallas_call(
        _pool_kernel,
        grid=(b, s // tm),
        in_specs=[pl.BlockSpec((1, tm, d), lambda bi, j: (bi, j, 0)),
                  pl.BlockSpec((1, d), lambda bi, j: (0, 0)),
                  pl.BlockSpec((d, d), lambda bi, j: (0, 0)),
                  pl.BlockSpec((ng, gw, gw), lambda bi, j: (0, 0, 0)),
                  pl.BlockSpec((d, d), lambda bi, j: (0, 0)),
                  pl.BlockSpec((1, d), lambda bi, j: (0, 0))],
        out_specs=pl.BlockSpec((1, tm, d), lambda bi, j: (bi, j, 0)),
        out_shape=jax.ShapeDtypeStruct((b, s, d), F32),
        scratch_shapes=[pltpu.VMEM((POOL_HALO + tm, d), F32),
                        pltpu.VMEM((tm, d), BF16)],
        compiler_params=_params("arbitrary", "arbitrary"),
        name="pool_mixer",
    )(x, gain, w_in, w_group, w_out, scale)


def _router_kernel(x_ref, g_ref, wr_ref, br_ref, o_ref, cnt_ref, carry_ref):
    tm = x_ref.shape[0]

    @pl.when(pl.program_id(0) == 0)
    def _():
        carry_ref[...] = jnp.zeros(carry_ref.shape, F32)

    h = _rms(x_ref[...], g_ref[...])
    h_hi = h.astype(BF16)
    h_lo = (h - h_hi.astype(F32)).astype(BF16)
    w = wr_ref[...]
    half = ROUTER_COLS // 2
    nt = (((1,), (1,)), ((), ()))
    a = lax.dot_general(w, h_hi, nt, preferred_element_type=F32)
    b = lax.dot_general(w, h_lo, nt, preferred_element_type=F32)
    lt = a[:half] + a[half:] + b[:half] + br_ref[...]

    def row(r):
        return lt[r:r + 1, :]

    g_best = row(N_EXPERTS)
    g_idx = jnp.zeros((1, tm), I32)
    for g in range(1, N_EXPERT_GROUPS):
        v = row(N_EXPERTS + g)
        better = v > g_best
        g_best = jnp.where(better, v, g_best)
        g_idx = jnp.where(better, g, g_idx)
    denom = jnp.zeros((1, tm), F32)
    for g in range(N_EXPERT_GROUPS):
        denom = denom + jnp.exp(row(N_EXPERTS + g) - g_best)
    g_prob = 1.0 / denom

    sel = []
    for e in range(EXPERTS_PER_GROUP):
        v = row(e)
        for g in range(1, N_EXPERT_GROUPS):
            v = jnp.where(g_idx == g, row(g * EXPERTS_PER_GROUP + e), v)
        sel.append(v)

    def top1(vals):
        best, idx = vals[0], jnp.zeros((1, tm), I32)
        for e in range(1, EXPERTS_PER_GROUP):
            better = vals[e] > best
            best = jnp.where(better, vals[e], best)
            idx = jnp.where(better, e, idx)
        return best, idx

    v0, i0 = top1(sel)
    v1, i1 = top1([jnp.where(i0 == e, -jnp.inf, sel[e]) for e in range(EXPERTS_PER_GROUP)])
    t = jnp.exp(v1 - v0)
    w0 = g_prob / (1.0 + t)
    w1 = w0 * t
    e0 = g_idx * EXPERTS_PER_GROUP + i0
    e1 = g_idx * EXPERTS_PER_GROUP + i1

    eid = lax.broadcasted_iota(I32, (N_EXPERTS, tm), 0)
    oh0 = eid == e0
    oh1 = eid == e1
    onehot = jnp.where(oh0 | oh1, 1.0, 0.0)
    earlier = (lax.broadcasted_iota(I32, (tm, tm), 0) < lax.broadcasted_iota(I32, (tm, tm), 1))
    prefix = jnp.dot(onehot.astype(BF16), jnp.where(earlier, 1.0, 0.0).astype(BF16),
                     preferred_element_type=F32)
    before = prefix + carry_ref[:, 0:1]
    rank0 = jnp.sum(jnp.where(oh0, before, 0.0), axis=0, keepdims=True)
    rank1 = jnp.sum(jnp.where(oh1, before, 0.0), axis=0, keepdims=True)
    carry_ref[...] = carry_ref[...] + jnp.sum(onehot, axis=1, keepdims=True)
    cnt_ref[...] = carry_ref[...]

    o_ref[...] = jnp.concatenate(
        [e0.astype(F32), e1.astype(F32), w0, w1, rank0, rank1, jnp.zeros((2, tm), F32)], axis=0)


def _router(x2, gain, wr, br):
    t, d = x2.shape
    tm = min(ROW_TILE, t)
    assert t * TOP_K < 2 ** 24
    return pl.pallas_call(
        _router_kernel,
        grid=(t // tm,),
        in_specs=[pl.BlockSpec((tm, d), lambda i: (i, 0)),
                  pl.BlockSpec((1, d), lambda i: (0, 0)),
                  pl.BlockSpec((ROUTER_COLS, d), lambda i: (0, 0)),
                  pl.BlockSpec((ROUTER_COLS // 2, 1), lambda i: (0, 0))],
        out_specs=[pl.BlockSpec((8, tm), lambda i: (0, i)),
                   pl.BlockSpec((N_EXPERTS, LANES), lambda i: (0, 0))],
        out_shape=[jax.ShapeDtypeStruct((8, t), F32),
                   jax.ShapeDtypeStruct((N_EXPERTS, LANES), F32)],
        scratch_shapes=[pltpu.VMEM((N_EXPERTS, LANES), F32)],
        compiler_params=_params("arbitrary"),
        name="router",
    )(x2, gain, wr, br)


def _slot_positions(route, counts_f, blk, n_blocks):
    e = route[0:TOP_K].astype(I32).T.reshape(-1)
    rank = route[2 * TOP_K:3 * TOP_K].astype(I32).T.reshape(-1)
    counts = counts_f[:, 0].astype(I32)
    padded = (counts + blk - 1) // blk * blk
    pad_end = jnp.cumsum(padded)
    pad_start = pad_end - padded
    experts = jnp.arange(N_EXPERTS, dtype=I32)
    pos = rank + jnp.sum(jnp.where(e[:, None] == experts[None, :], pad_start[None, :], 0), axis=1)
    first_row = jnp.arange(n_blocks, dtype=I32) * blk
    block_expert = jnp.minimum(
        jnp.sum((pad_end[None, :] <= first_row[:, None]).astype(I32), axis=1), N_EXPERTS - 1)
    n_used = (pad_end[-1] // blk).reshape(1)
    last_blocks = jnp.where(padded > 0, pad_end - blk, -1)
    trailing = n_used[0] + jnp.arange(N_EXPERTS, dtype=I32)
    trailing = jnp.where(trailing < n_blocks, trailing * blk, -1)
    zero_rows = jnp.concatenate([last_blocks, trailing])
    return pos.astype(I32), block_expert.astype(I32), n_used.astype(I32), zero_rows.astype(I32)


def _row_copy(src_ref, src_row, dst_ref, dst_row, sem):
    return pltpu.make_async_copy(src_ref.at[src_row], dst_ref.at[dst_row], sem)


def _dispatch_kernel(pos_ref, zrow_ref, x_ref, g_ref, xs_ref, buf_ref, zero_ref, sem, zsem):
    i = pl.program_id(0)
    n = pl.num_programs(0)
    tm = x_ref.shape[0]
    slot = i % 2

    @pl.when(i == 0)
    def _():
        zero_ref[...] = jnp.zeros(zero_ref.shape, U32)
        blk = zero_ref.shape[0]

        def zero_copy(j):
            row = pl.multiple_of(jnp.maximum(zrow_ref[j], 0), blk)
            return pltpu.make_async_copy(zero_ref, xs_ref.at[pl.ds(row, blk)], zsem)

        for j in range(zrow_ref.shape[0]):
            @pl.when(zrow_ref[j] >= 0)
            def _():
                zero_copy(j).start()
        for j in range(zrow_ref.shape[0]):
            @pl.when(zrow_ref[j] >= 0)
            def _():
                zero_copy(j).wait()

    def wait_slot(sl):
        for _ in range(TOP_K):
            pltpu.make_async_copy(buf_ref.at[sl], xs_ref.at[pl.ds(0, tm)], sem.at[sl]).wait()

    @pl.when(i >= 2)
    def _():
        wait_slot(slot)

    packed = _pack_rows(_rms(x_ref[...], g_ref[...]))
    buf_ref[slot] = packed.reshape(tm, packed.shape[1] // LANES, LANES)

    def issue(g, c):
        base = g * ISSUE_UNROLL
        for u in range(ISSUE_UNROLL):
            for k in range(TOP_K):
                p = pos_ref[(i * tm + base + u) * TOP_K + k]
                _row_copy(buf_ref.at[slot], base + u, xs_ref, p, sem.at[slot]).start()
        return c

    lax.fori_loop(0, tm // ISSUE_UNROLL, issue, 0)

    @pl.when(i == n - 1)
    def _():
        wait_slot(slot)

    @pl.when((i == n - 1) & (n >= 2))
    def _():
        wait_slot(1 - slot)


def _dispatch(pos, zero_rows, x2, gain, n_slots):
    t, d = x2.shape
    tm = min(MOVE_TILE, t)
    chunks = d // 2 // LANES
    grid_spec = pltpu.PrefetchScalarGridSpec(
        num_scalar_prefetch=2,
        grid=(t // tm,),
        in_specs=[pl.BlockSpec((tm, d), lambda i, pos, zr: (i, 0)),
                  pl.BlockSpec((1, d), lambda i, pos, zr: (0, 0))],
        out_specs=pl.BlockSpec(memory_space=pl.ANY),
        scratch_shapes=[pltpu.VMEM((2, tm, chunks, LANES), U32),
                        pltpu.VMEM((EXPERT_BLOCK, chunks, LANES), U32),
                        pltpu.SemaphoreType.DMA((2,)),
                        pltpu.SemaphoreType.DMA],
    )
    return pl.pallas_call(
        _dispatch_kernel,
        grid_spec=grid_spec,
        out_shape=jax.ShapeDtypeStruct((n_slots, chunks, LANES), U32),
        compiler_params=_params("arbitrary"),
        name="dispatch",
    )(pos, zero_rows, x2, gain)


def _expert_kernel(be_ref, nu_ref, xs_ref, wg_ref, wu_ref, wd_ref, y_ref, wgb_ref, wub_ref, wdb_ref):
    i = pl.program_id(0)
    e = be_ref[i]
    prev = be_ref[jnp.maximum(i - 1, 0)]

    @pl.when((i == 0) | (e != prev))
    def _():
        wgb_ref[...] = wg_ref[0, 0].astype(BF16)
        wub_ref[...] = wu_ref[0, 0].astype(BF16)
        wdb_ref[...] = wd_ref[0, 0].astype(BF16)

    @pl.when(i < nu_ref[0])
    def _():
        blk, chunks, lanes = xs_ref.shape
        x = _unpack_rows(xs_ref[...].reshape(blk, chunks * lanes)).astype(BF16)
        g = jnp.dot(x, wgb_ref[...], preferred_element_type=F32)
        u = jnp.dot(x, wub_ref[...], preferred_element_type=F32)
        a = (g / (1.0 + jnp.exp(-g)) * u).astype(BF16)
        y = _pack_rows(jnp.dot(a, wdb_ref[...], preferred_element_type=F32))
        y_ref[...] = y.reshape(blk, chunks, lanes)

    @pl.when(i >= nu_ref[0])
    def _():
        y_ref[...] = jnp.zeros(y_ref.shape, U32)


def _experts(block_expert, n_used, xs, w_gate, w_up, w_down, layer):
    n_slots, chunks, lanes = xs.shape
    blk = EXPERT_BLOCK
    d, ff = w_gate.shape[2], w_gate.shape[3]
    grid_spec = pltpu.PrefetchScalarGridSpec(
        num_scalar_prefetch=2,
        grid=(n_slots // blk,),
        in_specs=[pl.BlockSpec((blk, chunks, lanes), lambda i, be, nu: (i, 0, 0)),
                  pl.BlockSpec((1, 1, d, ff), lambda i, be, nu: (layer, be[i], 0, 0)),
                  pl.BlockSpec((1, 1, d, ff), lambda i, be, nu: (layer, be[i], 0, 0)),
                  pl.BlockSpec((1, 1, ff, d), lambda i, be, nu: (layer, be[i], 0, 0))],
        out_specs=pl.BlockSpec((blk, chunks, lanes), lambda i, be, nu: (i, 0, 0)),
        scratch_shapes=[pltpu.VMEM((d, ff), BF16),
                        pltpu.VMEM((d, ff), BF16),
                        pltpu.VMEM((ff, d), BF16)],
    )
    return pl.pallas_call(
        _expert_kernel,
        grid_spec=grid_spec,
        out_shape=jax.ShapeDtypeStruct((n_slots, chunks, lanes), U32),
        compiler_params=_params("arbitrary"),
        name="experts",
    )(block_expert, n_used, xs, w_gate, w_up, w_down)


def _combine_kernel(pos_ref, y_ref, x_ref, w_ref, o_ref, buf_ref, sem):
    i = pl.program_id(0)
    n = pl.num_programs(0)
    tm = x_ref.shape[0]
    slot = i % 2

    def issue_tile(tile, sl):
        def body(g, c):
            base = pl.multiple_of(g * ISSUE_UNROLL, ISSUE_UNROLL)
            for u in range(ISSUE_UNROLL):
                for k in range(TOP_K):
                    p = pos_ref[(tile * tm + base + u) * TOP_K + k]
                    _row_copy(y_ref, p, buf_ref.at[sl], k * tm + base + u, sem.at[sl]).start()
            return c
        lax.fori_loop(0, tm // ISSUE_UNROLL, body, 0)

    @pl.when(i == 0)
    def _():
        issue_tile(0, 0)

    @pl.when(i + 1 < n)
    def _():
        issue_tile(i + 1, 1 - slot)

    pltpu.make_async_copy(y_ref.at[pl.ds(0, TOP_K * tm)], buf_ref.at[slot], sem.at[slot]).wait()

    slabs = buf_ref[slot]
    rows = slabs.reshape(slabs.shape[0], slabs.shape[1] * slabs.shape[2])
    w = w_ref[...]
    out = x_ref[...]
    for k in range(TOP_K):
        out = out + w[:, k:k + 1] * _unpack_rows(rows[k * tm:(k + 1) * tm])
    o_ref[...] = out


def _combine(pos, y, x2, weights):
    t, d = x2.shape
    tm = min(MOVE_TILE, t)
    chunks = d // 2 // LANES
    grid_spec = pltpu.PrefetchScalarGridSpec(
        num_scalar_prefetch=1,
        grid=(t // tm,),
        in_specs=[pl.BlockSpec(memory_space=pl.ANY),
                  pl.BlockSpec((tm, d), lambda i, pos: (i, 0)),
                  pl.BlockSpec((tm, TOP_K), lambda i, pos: (i, 0))],
        out_specs=pl.BlockSpec((tm, d), lambda i, pos: (i, 0)),
        scratch_shapes=[pltpu.VMEM((2, TOP_K * tm, chunks, LANES), U32),
                        pltpu.SemaphoreType.DMA((2,))],
    )
    return pl.pallas_call(
        _combine_kernel,
        grid_spec=grid_spec,
        out_shape=jax.ShapeDtypeStruct((t, d), F32),
        compiler_params=_params("arbitrary"),
        name="combine",
    )(pos, y, x2, weights)


def _hier_moe(x2, gain, wg1, bg1, wg2, bg2, w_gate, w_up, w_down, layer):
    t, d = x2.shape
    half = ROUTER_COLS // 2
    pad = half - N_EXPERTS - N_EXPERT_GROUPS
    w_f32 = jnp.concatenate([wg2.astype(F32), wg1.astype(F32), jnp.zeros((d, pad), F32)], axis=1)
    w_hi = w_f32.astype(BF16)
    w_lo = (w_f32 - w_hi.astype(F32)).astype(BF16)
    wr = jnp.concatenate([w_hi, w_lo], axis=1).T
    br = jnp.concatenate([bg2.astype(F32), bg1.astype(F32), jnp.zeros((pad,), F32)]).reshape(half, 1)
    route, counts = _router(x2, gain, wr, br)

    blk = EXPERT_BLOCK
    n_blocks = -(-(t * TOP_K) // blk) + N_EXPERTS
    pos, block_expert, n_used, zero_rows = _slot_positions(route, counts, blk, n_blocks)
    xs = _dispatch(pos, zero_rows, x2, gain, n_blocks * blk)
    y = _experts(block_expert, n_used, xs, w_gate, w_up, w_down, layer)
    return _combine(pos, y, x2, route[TOP_K:2 * TOP_K].T)


def kernel(x, rel_bias, attn_norm, w_qkv, q_gain, k_gain, lambda_q1, lambda_k1, lambda_q2, lambda_k2,
           subln_gain, w_o, pool_norm, pool_w_in, pool_w_group, pool_w_out, pool_scale,
           ffn_norm, router_group_w, router_group_b, router_expert_w, router_expert_b,
           w_gate, w_up, w_down):
    b, s, d = x.shape
    depth = ffn_norm.shape[0]
    x = x.astype(F32)
    bias_tiles = _bias_tiles(rel_bias)
    for i in range(depth):
        j = i // N_MIXERS
        if i % N_MIXERS == 0:
            lambda_init = 0.8 - 0.6 * math.exp(-0.3 * i)
            x2 = x.reshape(b * s, d)
            qkv = _norm_matmul(x2, attn_norm[j].reshape(1, d).astype(F32), w_qkv[j].astype(BF16))
            lam_params = jnp.stack([lambda_q1[j], lambda_k1[j], lambda_q2[j], lambda_k2[j]]).astype(F32)
            o = _attention(qkv.reshape(-1, b, s, HEAD_W), bias_tiles,
                           jnp.tile(q_gain[j].astype(F32), 2).reshape(HEAD_W, 1),
                           jnp.tile(k_gain[j].astype(F32), 2).reshape(HEAD_W, 1),
                           lam_params, subln_gain[j].reshape(HEAD_W, 1).astype(F32), lambda_init)
            x2 = _matmul_residual(o.reshape(-1, b * s, HEAD_W), w_o[j].astype(BF16), x2)
        else:
            x3 = _pool_mixer(x, pool_norm[j].reshape(1, d).astype(F32), pool_w_in[j].astype(BF16),
                             pool_w_group[j].astype(BF16), pool_w_out[j].astype(BF16),
                             pool_scale[j].reshape(1, d).astype(F32))
            x2 = x3.reshape(b * s, d)
        x2 = _hier_moe(x2, ffn_norm[i].reshape(1, d).astype(F32), router_group_w[i], router_group_b[i],
                       router_expert_w[i], router_expert_b[i], w_gate, w_up, w_down, i)
        x = x2.reshape(b, s, d)
    return x
```
